```python
import jax, jax.numpy as jnp
from jax import lax
import numpy as np

D_MODEL = 4096
BATCH = 4
SEQ = 4096
DEPTH = 4

BRANCH_WIDTH = D_MODEL // 4
N_BRANCHES = 3
POOL_WINDOWS = (2, 4, 8, 16)
POOL_GROUPS = 4
POOL_GROUP_DIM = BRANCH_WIDTH // POOL_GROUPS
SGU_CHUNK = 128
SGU_HEADS = 4
SGU_HEAD_DIM = BRANCH_WIDTH // SGU_HEADS
CONV_WIDTH = 3
IN_COLS = 6 * BRANCH_WIDTH + N_BRANCHES * D_MODEL
D_FF = (3 * D_MODEL) // 2
N_EXPERTS = 8
TOP_K = 2
D_FF_EXPERT = D_FF // 4
N_DENSE = (DEPTH + 1) // 2
N_MOE = DEPTH // 2
EPS = 1e-6

kernel_name = "hybrid_pool_sgu_conv_moe_trunk"


def rms_norm(x, g):
    xf = x.astype(jnp.float32)
    y = xf * lax.rsqrt(jnp.mean(xf * xf, axis=-1, keepdims=True) + EPS)
    return (y * g.astype(jnp.float32)).astype(x.dtype)


def pool_mixer(a, pool_w, pool_scale):
    bsz, seq, _ = a.shape
    af = a.astype(jnp.float32).reshape(bsz, seq, POOL_GROUPS, POOL_GROUP_DIM)
    cs = jnp.cumsum(af, axis=1)
    pos = jnp.arange(seq)
    outs = []
    for g, w in enumerate(POOL_WINDOWS):
        c = cs[:, :, g]
        lag = jnp.pad(c[:, :seq - w], ((0, 0), (w, 0), (0, 0)))
        cnt = jnp.minimum(pos + 1, w).astype(jnp.float32)[None, :, None]
        outs.append((c - lag) / cnt - af[:, :, g])
    pooled = jnp.stack(outs, axis=2).astype(a.dtype)
    y = jnp.einsum('bsgc,gce->bsge', pooled, pool_w).reshape(bsz, seq, BRANCH_WIDTH)
    return y * pool_scale


def sgu_mixer(zu, zv, norm_g, ws, b):
    bsz, seq, _ = zu.shape
    n_chunks = seq // SGU_CHUNK
    u = jax.nn.gelu(zu, approximate=False)
    v = rms_norm(jax.nn.gelu(zv, approximate=False), norm_g)
    v = v.reshape(bsz, n_chunks, SGU_CHUNK, SGU_HEADS, SGU_HEAD_DIM)
    mask = jnp.tril(jnp.ones((SGU_CHUNK, SGU_CHUNK), dtype=ws.dtype))
    w_causal = ws * mask[None]
    vm = jnp.einsum('gts,bnsgc->bntgc', w_causal, v) + b.T[None, None, :, :, None]
    return u * vm.reshape(bsz, seq, BRANCH_WIDTH)


def conv_mixer(xc, bg, cg, conv_w):
    h = cg * xc
    y = lax.conv_general_dilated(
        h, conv_w[:, None, :].astype(h.dtype), window_strides=(1,),
        padding=[(CONV_WIDTH - 1, 0)],
        dimension_numbers=('NWC', 'WIO', 'NWC'),
        feature_group_count=BRANCH_WIDTH)
    return bg * y


def mixer_block(xn, w_in, pool_w, pool_scale, sgu_norm, sgu_ws, sgu_b, conv_w, w_branch, w_out):
    proj = jnp.einsum('bsd,dk->bsk', xn, w_in)
    W = BRANCH_WIDTH
    splits = [W, 2 * W, 3 * W, 4 * W, 5 * W, 6 * W, 6 * W + D_MODEL, 6 * W + 2 * D_MODEL]
    a, zu, zv, xc, bg, cg, g_a, g_b, g_c = jnp.split(proj, splits, axis=-1)
    y_a = pool_mixer(a, pool_w, pool_scale)
    y_b = sgu_mixer(zu, zv, sgu_norm, sgu_ws, sgu_b)
    y_c = conv_mixer(xc, bg, cg, conv_w)
    merged = (jax.nn.sigmoid(g_a) * jnp.einsum('bsc,cd->bsd', y_a, w_branch[0])
              + jax.nn.sigmoid(g_b) * jnp.einsum('bsc,cd->bsd', y_b, w_branch[1])
              + jax.nn.sigmoid(g_c) * jnp.einsum('bsc,cd->bsd', y_c, w_branch[2]))
    return jnp.einsum('bsd,de->bse', merged, w_out)


def swiglu(x, w_gate, w_up, w_down):
    return (jax.nn.silu(x @ w_gate) * (x @ w_up)) @ w_down


def moe_swiglu(x, router_w, router_b, w_gate, w_up, w_down):
    bsz, seq, d = x.shape
    t = x.reshape(-1, d)
    logits = (t @ router_w).astype(jnp.float32) + router_b.astype(jnp.float32)
    top_v, top_i = lax.top_k(logits, TOP_K)
    probs = jax.nn.softmax(top_v, axis=-1)
    combine = jnp.sum(jax.nn.one_hot(top_i, N_EXPERTS, dtype=jnp.float32) * probs[..., None], axis=1)
    combine = combine.astype(x.dtype)
    out = jnp.zeros_like(t)
    for e in range(N_EXPERTS):
        out = out + combine[:, e:e + 1] * swiglu(t, w_gate[e], w_up[e], w_down[e])
    return out.reshape(bsz, seq, d)


def setup_inputs(seed: int = 0) -> dict:
    key = jax.random.key(seed)
    ks = jax.random.split(key, 24)

    def nrm(k, shape, scale):
        return jax.random.normal(k, shape, jnp.float32) * scale

    def gain(k, shape):
        return 1.0 + 0.02 * jax.random.normal(k, shape, jnp.float32)

    return {
        "x": nrm(ks[0], (BATCH, SEQ, D_MODEL), 1.0),
        "norm_mix": gain(ks[1], (DEPTH, D_MODEL)),
        "w_in": nrm(ks[2], (DEPTH, D_MODEL, IN_COLS), D_MODEL ** -0.5),
        "pool_w": nrm(ks[3], (DEPTH, POOL_GROUPS, POOL_GROUP_DIM, POOL_GROUP_DIM), POOL_GROUP_DIM ** -0.5),
        "pool_scale": gain(ks[4], (DEPTH, BRANCH_WIDTH)),
        "sgu_norm": gain(ks[5], (DEPTH, BRANCH_WIDTH)),
        "sgu_ws": nrm(ks[6], (DEPTH, SGU_HEADS, SGU_CHUNK, SGU_CHUNK), SGU_CHUNK ** -0.5),
        "sgu_b": gain(ks[7], (DEPTH, SGU_HEADS, SGU_CHUNK)),
        "conv_w": nrm(ks[8], (DEPTH, CONV_WIDTH, BRANCH_WIDTH), CONV_WIDTH ** -0.5),
        "w_branch": nrm(ks[9], (DEPTH, N_BRANCHES, BRANCH_WIDTH, D_MODEL), BRANCH_WIDTH ** -0.5),
        "w_out": nrm(ks[10], (DEPTH, D_MODEL, D_MODEL), D_MODEL ** -0.5),
        "norm_ffn": gain(ks[11], (DEPTH, D_MODEL)),
        "dense_w_gate": nrm(ks[12], (N_DENSE, D_MODEL, D_FF), D_MODEL ** -0.5),
        "dense_w_up": nrm(ks[13], (N_DENSE, D_MODEL, D_FF), D_MODEL ** -0.5),
        "dense_w_down": nrm(ks[14], (N_DENSE, D_FF, D_MODEL), D_FF ** -0.5),
        "router_w": nrm(ks[15], (N_MOE, D_MODEL, N_EXPERTS), D_MODEL ** -0.5),
        "router_b": nrm(ks[16], (N_MOE, N_EXPERTS), 0.01),
        "moe_w_gate": nrm(ks[17], (N_MOE, N_EXPERTS, D_MODEL, D_FF_EXPERT), D_MODEL ** -0.5),
        "moe_w_up": nrm(ks[18], (N_MOE, N_EXPERTS, D_MODEL, D_FF_EXPERT), D_MODEL ** -0.5),
        "moe_w_down": nrm(ks[19], (N_MOE, N_EXPERTS, D_FF_EXPERT, D_MODEL), D_FF_EXPERT ** -0.5),
        "norm_final": gain(ks[20], (D_MODEL,)),
    }


def reference(x, norm_mix, w_in, pool_w, pool_scale, sgu_norm, sgu_ws, sgu_b, conv_w,
              w_branch, w_out, norm_ffn, dense_w_gate, dense_w_up, dense_w_down,
              router_w, router_b, moe_w_gate, moe_w_up, moe_w_down, norm_final):
    h = x
    for layer in range(DEPTH):
        xn = rms_norm(h, norm_mix[layer])
        h = h + mixer_block(xn, w_in[layer], pool_w[layer], pool_scale[layer], sgu_norm[layer],
                            sgu_ws[layer], sgu_b[layer], conv_w[layer], w_branch[layer], w_out[layer])
        hn = rms_norm(h, norm_ffn[layer])
        j = layer // 2
        if layer % 2 == 0:
            h = h + swiglu(hn, dense_w_gate[j], dense_w_up[j], dense_w_down[j])
        else:
            h = h + moe_swiglu(hn, router_w[j], router_b[j], moe_w_gate[j], moe_w_up[j], moe_w_down[j])
    return rms_norm(h, norm_final)
```

```python
import functools

import jax
import jax.numpy as jnp
from jax import lax
from jax.experimental import pallas as pl
from jax.experimental.pallas import tpu as pltpu

D_MODEL = 4096
DEPTH = 4
BRANCH_WIDTH = D_MODEL // 4
N_BRANCHES = 3
POOL_WINDOWS = (2, 4, 8, 16)
POOL_GROUPS = 4
POOL_GROUP_DIM = BRANCH_WIDTH // POOL_GROUPS
SGU_CHUNK = 128
SGU_HEADS = 4
SGU_HEAD_DIM = BRANCH_WIDTH // SGU_HEADS
CONV_WIDTH = 3
MIX_COLS = 6 * BRANCH_WIDTH
D_FF = (3 * D_MODEL) // 2
N_EXPERTS = 8
TOP_K = 2
D_FF_EXPERT = D_FF // 4
EPS = 1e-6

V7X_VMEM_LIMIT_BYTES = 58 * 1024 * 1024
HALO_ROWS = 16

BF16 = jnp.bfloat16
F32 = jnp.float32


def _params(*semantics):
    return pltpu.CompilerParams(dimension_semantics=semantics,
                                vmem_limit_bytes=V7X_VMEM_LIMIT_BYTES)


def _rms(x, g):
    ms = jnp.mean(x * x, axis=-1, keepdims=True)
    return x * lax.rsqrt(ms + EPS) * g


def _rmsnorm_kernel(x_ref, g_ref, o_ref):
    o_ref[...] = _rms(x_ref[...], g_ref[...]).astype(o_ref.dtype)


def rmsnorm(x, g, out_dtype, tr=256):
    n, d = x.shape
    return pl.pallas_call(
        _rmsnorm_kernel,
        grid=(n // tr,),
        in_specs=[pl.BlockSpec((tr, d), lambda i: (i, 0)),
                  pl.BlockSpec((1, d), lambda i: (0, 0))],
        out_specs=pl.BlockSpec((tr, d), lambda i: (i, 0)),
        out_shape=jax.ShapeDtypeStruct((n, d), out_dtype),
        compiler_params=_params("arbitrary"),
    )(x, g.reshape(1, d))


def _rmsnorm_router_kernel(x_ref, g_ref, rw_ref, rb_ref, o_ref, c_ref):
    hn = _rms(x_ref[...], g_ref[...])
    o_ref[...] = hn.astype(o_ref.dtype)
    logits = jnp.dot(hn, rw_ref[...], precision=lax.Precision.HIGHEST,
                     preferred_element_type=F32) + rb_ref[...]
    idx = lax.broadcasted_iota(jnp.int32, logits.shape, 1)
    m1 = jnp.max(logits, axis=1, keepdims=True)
    i1 = jnp.min(jnp.where(logits == m1, idx, N_EXPERTS), axis=1, keepdims=True)
    sel1 = idx == i1
    rest = jnp.where(sel1, -jnp.inf, logits)
    m2 = jnp.max(rest, axis=1, keepdims=True)
    i2 = jnp.min(jnp.where((rest == m2) & jnp.logical_not(sel1), idx, N_EXPERTS),
                 axis=1, keepdims=True)
    sel2 = idx == i2
    e2 = jnp.exp(m2 - m1)
    denom = 1.0 + e2
    c_ref[...] = jnp.where(sel1, 1.0 / denom, 0.0) + jnp.where(sel2, e2 / denom, 0.0)


def rmsnorm_router(x, g, router_w, router_b, tr=256):
    n, d = x.shape
    return pl.pallas_call(
        _rmsnorm_router_kernel,
        grid=(n // tr,),
        in_specs=[pl.BlockSpec((tr, d), lambda i: (i, 0)),
                  pl.BlockSpec((1, d), lambda i: (0, 0)),
                  pl.BlockSpec((d, N_EXPERTS), lambda i: (0, 0)),
                  pl.BlockSpec((1, N_EXPERTS), lambda i: (0, 0))],
        out_specs=[pl.BlockSpec((tr, d), lambda i: (i, 0)),
                   pl.BlockSpec((tr, N_EXPERTS), lambda i: (i, 0))],
        out_shape=[jax.ShapeDtypeStruct((n, d), BF16),
                   jax.ShapeDtypeStruct((n, N_EXPERTS), F32)],
        compiler_params=_params("arbitrary"),
    )(x, g.reshape(1, d), router_w, router_b.reshape(1, N_EXPERTS))


def _matmul_kernel(x_ref, w_ref, o_ref):
    o_ref[...] = jnp.dot(x_ref[...], w_ref[...],
                         preferred_element_type=F32).astype(o_ref.dtype)


def matmul_cols(x, w, n_cols, out_dtype, tm=1024, tn=1024):
    m, k = x.shape
    return pl.pallas_call(
        _matmul_kernel,
        grid=(n_cols // tn, m // tm),
        in_specs=[pl.BlockSpec((tm, k), lambda j, i: (i, 0)),
                  pl.BlockSpec((k, tn), lambda j, i: (0, j))],
        out_specs=pl.BlockSpec((tm, tn), lambda j, i: (i, j)),
        out_shape=jax.ShapeDtypeStruct((m, n_cols), out_dtype),
        compiler_params=_params("arbitrary", "arbitrary"),
    )(x, w)


def _matmul_residual_kernel(x_ref, w_ref, r_ref, o_ref):
    o_ref[...] = r_ref[...] + jnp.dot(x_ref[...], w_ref[...], preferred_element_type=F32)


def matmul_residual(x, w, res, k_block=0, tm=1024, tn=512):
    m = x.shape[0]
    k, n = w.shape
    return pl.pallas_call(
        _matmul_residual_kernel,
        grid=(n // tn, m // tm),
        in_specs=[pl.BlockSpec((tm, k), lambda j, i: (i, k_block)),
                  pl.BlockSpec((k, tn), lambda j, i: (0, j)),
                  pl.BlockSpec((tm, tn), lambda j, i: (i, j))],
        out_specs=pl.BlockSpec((tm, tn), lambda j, i: (i, j)),
        out_shape=jax.ShapeDtypeStruct((m, n), F32),
        input_output_aliases={2: 0},
        compiler_params=_params("arbitrary", "arbitrary"),
    )(x, w, res)


def _gelu(x):
    return 0.5 * x * (1.0 + lax.erf(x * (2.0 ** -0.5)))


def _shift_rows(x, k):
    return pltpu.roll(x, k, axis=0)


def _mixer_kernel(p_ref, poolw_ref, pscale_ref, sgunorm_ref, ws_ref, sb_ref, convw_ref,
                  y_ref, halo_ref, *, ts):
    w = BRANCH_WIDTH
    s_idx = pl.program_id(1)

    @pl.when(s_idx == 0)
    def _():
        halo_ref[...] = jnp.zeros_like(halo_ref)

    a = p_ref[:, 0:w].astype(F32)
    ext_a = jnp.concatenate([halo_ref[:, 0:w], a], axis=0)
    pos = s_idx * ts + lax.broadcasted_iota(jnp.int32, (ts, 1), 0)
    for g, win in enumerate(POOL_WINDOWS):
        cols = slice(g * POOL_GROUP_DIM, (g + 1) * POOL_GROUP_DIM)
        acc = ext_a[:, cols]
        step = 1
        while step < win:
            acc = acc + _shift_rows(acc, step)
            step *= 2
        cnt = jnp.minimum(pos + 1, win).astype(F32)
        pooled = acc[HALO_ROWS:, :] / cnt - a[:, cols]
        ya = jnp.dot(pooled.astype(BF16), poolw_ref[g], preferred_element_type=F32)
        y_ref[:, cols] = (ya * pscale_ref[:, cols]).astype(y_ref.dtype)

    u = _gelu(p_ref[:, w:2 * w].astype(F32))
    v = _rms(_gelu(p_ref[:, 2 * w:3 * w].astype(F32)), sgunorm_ref[...]).astype(BF16)
    row = lax.broadcasted_iota(jnp.int32, (SGU_CHUNK, SGU_CHUNK), 0)
    col = lax.broadcasted_iota(jnp.int32, (SGU_CHUNK, SGU_CHUNK), 1)
    for g in range(SGU_HEADS):
        cols = slice(g * SGU_HEAD_DIM, (g + 1) * SGU_HEAD_DIM)
        w_causal = jnp.where(row >= col, ws_ref[g], 0.0).astype(BF16)
        bias = sb_ref[:, g:g + 1]
        for c in range(ts // SGU_CHUNK):
            rows = slice(c * SGU_CHUNK, (c + 1) * SGU_CHUNK)
            vm = jnp.dot(w_causal, v[rows, cols], preferred_element_type=F32) + bias
            y_ref[rows, w + g * SGU_HEAD_DIM:w + (g + 1) * SGU_HEAD_DIM] = (
                u[rows, cols] * vm).astype(y_ref.dtype)

    hc = p_ref[:, 5 * w:6 * w].astype(F32) * p_ref[:, 3 * w:4 * w].astype(F32)
    ext_h = jnp.concatenate([halo_ref[:, w:2 * w], hc], axis=0)
    yc = (convw_ref[2:3, :] * ext_h
          + convw_ref[1:2, :] * _shift_rows(ext_h, 1)
          + convw_ref[0:1, :] * _shift_rows(ext_h, 2))
    y_ref[:, 2 * w:3 * w] = (p_ref[:, 4 * w:5 * w].astype(F32)
                             * yc[HALO_ROWS:, :]).astype(y_ref.dtype)

    halo_ref[:, 0:w] = a[ts - HALO_ROWS:, :]
    halo_ref[:, w:2 * w] = hc[ts - HALO_ROWS:, :]


def token_mixers(p6, pool_w, pool_scale, sgu_norm, sgu_ws, sgu_b, conv_w, bsz, seq, ts=256):
    n = p6.shape[0]
    w = BRANCH_WIDTH
    tiles = seq // ts
    row_block = lambda b, s: (b * tiles + s, 0)
    const2 = lambda b, s: (0, 0)
    const3 = lambda b, s: (0, 0, 0)
    return pl.pallas_call(
        functools.partial(_mixer_kernel, ts=ts),
        grid=(bsz, tiles),
        in_specs=[pl.BlockSpec((ts, MIX_COLS), row_block),
                  pl.BlockSpec((POOL_GROUPS, POOL_GROUP_DIM, POOL_GROUP_DIM), const3),
                  pl.BlockSpec((1, w), const2),
                  pl.BlockSpec((1, w), const2),
                  pl.BlockSpec((SGU_HEADS, SGU_CHUNK, SGU_CHUNK), const3),
                  pl.BlockSpec((SGU_CHUNK, SGU_HEADS), const2),
                  pl.BlockSpec((CONV_WIDTH, w), const2)],
        out_specs=pl.BlockSpec((ts, N_BRANCHES * w), row_block),
        out_shape=jax.ShapeDtypeStruct((n, N_BRANCHES * w), BF16),
        scratch_shapes=[pltpu.VMEM((HALO_ROWS, 2 * w), F32)],
        compiler_params=_params("arbitrary", "arbitrary"),
    )(p6, pool_w.astype(BF16), pool_scale.reshape(1, w), sgu_norm.reshape(1, w),
      sgu_ws, sgu_b.T, conv_w)


def _gate_merge_kernel(xn_ref, y_ref, wga_ref, wgb_ref, wgc_ref, wb_ref, o_ref):
    w = BRANCH_WIDTH
    xn = xn_ref[...]
    acc = None
    for n, wg_ref in enumerate((wga_ref, wgb_ref, wgc_ref)):
        gate = jnp.dot(xn, wg_ref[...], preferred_element_type=F32)
        branch = jnp.dot(y_ref[:, n * w:(n + 1) * w], wb_ref[n], preferred_element_type=F32)
        term = jax.nn.sigmoid(gate) * branch
        acc = term if acc is None else acc + term
    o_ref[...] = acc.astype(o_ref.dtype)


def gate_merge(xn, y, w_in, w_branch, tm=512, tn=512):
    m, d = xn.shape
    w = BRANCH_WIDTH
    gate_col0 = MIX_COLS // tn
    gate_stride = D_MODEL // tn

    def gate_spec(n):
        return pl.BlockSpec((d, tn), lambda j, i: (0, gate_col0 + n * gate_stride + j))

    return pl.pallas_call(
        _gate_merge_kernel,
        grid=(D_MODEL // tn, m // tm),
        in_specs=[pl.BlockSpec((tm, d), lambda j, i: (i, 0)),
                  pl.BlockSpec((tm, N_BRANCHES * w), lambda j, i: (i, 0)),
                  gate_spec(0), gate_spec(1), gate_spec(2),
                  pl.BlockSpec((N_BRANCHES, w, tn), lambda j, i: (0, 0, j))],
        out_specs=pl.BlockSpec((tm, tn), lambda j, i: (i, j)),
        out_shape=jax.ShapeDtypeStruct((m, D_MODEL), BF16),
        compiler_params=_params("arbitrary", "arbitrary"),
    )(xn, y, w_in, w_in, w_in, w_branch)


def _swiglu_up_kernel(x_ref, wg_ref, wu_ref, o_ref):
    x = x_ref[...]
    gate = jnp.dot(x, wg_ref[...], preferred_element_type=F32)
    up = jnp.dot(x, wu_ref[...], preferred_element_type=F32)
    o_ref[...] = (jax.nn.silu(gate) * up).astype(o_ref.dtype)


def swiglu_up(x, w_gate, w_up, tm=1024, tn=512):
    m, k = x.shape
    n = w_gate.shape[1]
    w_spec = pl.BlockSpec((k, tn), lambda j, i: (0, j))
    return pl.pallas_call(
        _swiglu_up_kernel,
        grid=(n // tn, m // tm),
        in_specs=[pl.BlockSpec((tm, k), lambda j, i: (i, 0)), w_spec, w_spec],
        out_specs=pl.BlockSpec((tm, tn), lambda j, i: (i, j)),
        out_shape=jax.ShapeDtypeStruct((m, n), BF16),
        compiler_params=_params("arbitrary", "arbitrary"),
    )(x, w_gate, w_up)


def _moe_up_kernel(x_ref, c_ref, wg_ref, wu_ref, o_ref, *, tiles_per_expert):
    e = pl.program_id(0) // tiles_per_expert
    x = x_ref[...]
    gate = jnp.dot(x, wg_ref[...], preferred_element_type=F32)
    up = jnp.dot(x, wu_ref[...], preferred_element_type=F32)
    lane = lax.broadcasted_iota(jnp.int32, c_ref.shape, 1)
    c_e = jnp.sum(jnp.where(lane == e, c_ref[...], 0.0), axis=1, keepdims=True)
    o_ref[...] = (jax.nn.silu(gate) * up * c_e).astype(o_ref.dtype)


def moe_up(x, combine, w_gate, w_up, tm=1024, tn=512):
    m, k = x.shape
    tiles_per_expert = D_FF_EXPERT // tn
    w_spec = pl.BlockSpec((None, k, tn),
                          lambda j, i: (j // tiles_per_expert, 0, j % tiles_per_expert))
    return pl.pallas_call(
        functools.partial(_moe_up_kernel, tiles_per_expert=tiles_per_expert),
        grid=(N_EXPERTS * tiles_per_expert, m // tm),
        in_specs=[pl.BlockSpec((tm, k), lambda j, i: (i, 0)),
                  pl.BlockSpec((tm, N_EXPERTS), lambda j, i: (i, 0)),
                  w_spec, w_spec],
        out_specs=pl.BlockSpec((tm, tn), lambda j, i: (i, j)),
        out_shape=jax.ShapeDtypeStruct((m, N_EXPERTS * D_FF_EXPERT), BF16),
        compiler_params=_params("arbitrary", "arbitrary"),
    )(x, combine, w_gate, w_up)


def kernel(x, norm_mix, w_in, pool_w, pool_scale, sgu_norm, sgu_ws, sgu_b, conv_w, w_branch, w_out, norm_ffn, dense_w_gate, dense_w_up, dense_w_down, router_w, router_b, moe_w_gate, moe_w_up, moe_w_down, norm_final):
    bsz, seq, d = x.shape
    h = x.reshape(bsz * seq, d)
    for layer in range(DEPTH):
        w_in_l = w_in[layer].astype(BF16)
        xn = rmsnorm(h, norm_mix[layer], BF16)
        p6 = matmul_cols(xn, w_in_l, MIX_COLS, BF16)
        y = token_mixers(p6, pool_w[layer], pool_scale[layer], sgu_norm[layer], sgu_ws[layer],
                         sgu_b[layer], conv_w[layer], bsz, seq)
        merged = gate_merge(xn, y, w_in_l, w_branch[layer].astype(BF16))
        h = matmul_residual(merged, w_out[layer].astype(BF16), h)
        j = layer // 2
        if layer % 2 == 0:
            hn = rmsnorm(h, norm_ffn[layer], BF16)
            act = swiglu_up(hn, dense_w_gate[j].astype(BF16), dense_w_up[j].astype(BF16))
            h = matmul_residual(act, dense_w_down[j].astype(BF16), h)
        else:
            hn, combine = rmsnorm_router(h, norm_ffn[layer], router_w[j], router_b[j])
            act = moe_up(hn, combine, moe_w_gate[j].astype(BF16), moe_w_up[j].astype(BF16))
            w_down = moe_w_down[j].astype(BF16).reshape(N_EXPERTS * D_FF_EXPERT, d)
            half = w_down.shape[0] // 2
            h = matmul_residual(act, w_down[:half], h, k_block=0)
            h = matmul_residual(act, w_down[half:], h, k_block=1)
    return rmsnorm(h, norm_final, F32).reshape(bsz, seq, d)
```

```python
import functools

import jax
import jax.numpy as jnp
from jax import lax
from jax.experimental import pallas as pl
from jax.experimental.pallas import tpu as pltpu

D_MODEL = 4096
DEPTH = 4
BRANCH_WIDTH = D_MODEL // 4
N_BRANCHES = 3
POOL_WINDOWS = (2, 4, 8, 16)
POOL_GROUPS = 4
POOL_GROUP_DIM = BRANCH_WIDTH // POOL_GROUPS
SGU_CHUNK = 128
SGU_HEADS = 4
SGU_HEAD_DIM = BRANCH_WIDTH // SGU_HEADS
CONV_WIDTH = 3
MIX_COLS = 6 * BRANCH_WIDTH
D_FF = (3 * D_MODEL) // 2
N_EXPERTS = 8
TOP_K = 2
D_FF_EXPERT = D_FF // 4
EPS = 1e-6

V7X_VMEM_LIMIT_BYTES = 58 * 1024 * 1024
HALO_ROWS = 16

BF16 = jnp.bfloat16
F32 = jnp.float32


def _params(*semantics):
    return pltpu.CompilerParams(dimension_semantics=semantics,
                                vmem_limit_bytes=V7X_VMEM_LIMIT_BYTES)


def _rms(x, g):
    ms = jnp.mean(x * x, axis=-1, keepdims=True)
    return x * lax.rsqrt(ms + EPS) * g


def _rmsnorm_kernel(x_ref, g_ref, o_ref):
    o_ref[...] = _rms(x_ref[...], g_ref[...]).astype(o_ref.dtype)


def rmsnorm(x, g, out_dtype, tr=256):
    n, d = x.shape
    return pl.pallas_call(
        _rmsnorm_kernel,
        grid=(n // tr,),
        in_specs=[pl.BlockSpec((tr, d), lambda i: (i, 0)),
                  pl.BlockSpec((1, d), lambda i: (0, 0))],
        out_specs=pl.BlockSpec((tr, d), lambda i: (i, 0)),
        out_shape=jax.ShapeDtypeStruct((n, d), out_dtype),
        compiler_params=_params("arbitrary"),
    )(x, g.reshape(1, d))


MOE_TILE = 512
DMA_CHUNK = 256
HIGH_HALF = 0xFFFF0000


def _pack_bf16_pairs(x):
    half = x.shape[1] // 2
    hi = lax.bitcast_convert_type(x[:, :half].astype(BF16).astype(F32), jnp.uint32)
    lo = lax.bitcast_convert_type(x[:, half:].astype(BF16).astype(F32), jnp.uint32)
    return (hi & jnp.uint32(HIGH_HALF)) | (lo >> 16)


def _unpack_bf16_pairs(p):
    hi = lax.bitcast_convert_type(p & jnp.uint32(HIGH_HALF), F32)
    lo = lax.bitcast_convert_type(p << 16, F32)
    return hi, lo


def _rmsnorm_router_kernel(x_ref, g_ref, rw_ref, rb_ref, o_ref, code_ref, prob_ref, cnt_ref,
                           carry_ref, *, n_tokens):
    i = pl.program_id(0)
    tr = x_ref.shape[0]

    @pl.when(i == 0)
    def _():
        carry_ref[...] = jnp.zeros_like(carry_ref)

    hn = _rms(x_ref[...], g_ref[...])
    o_ref[...] = _pack_bf16_pairs(hn)
    logits = jnp.dot(hn, rw_ref[...], precision=lax.Precision.HIGHEST,
                     preferred_element_type=F32) + rb_ref[...]
    idx = lax.broadcasted_iota(jnp.int32, logits.shape, 1)
    m1 = jnp.max(logits, axis=1, keepdims=True)
    i1 = jnp.min(jnp.where(logits == m1, idx, N_EXPERTS), axis=1, keepdims=True)
    sel1 = idx == i1
    rest = jnp.where(sel1, -jnp.inf, logits)
    m2 = jnp.max(rest, axis=1, keepdims=True)
    i2 = jnp.min(jnp.where((rest == m2) & jnp.logical_not(sel1), idx, N_EXPERTS),
                 axis=1, keepdims=True)
    sel2 = idx == i2
    e2 = jnp.exp(m2 - m1)
    denom = 1.0 + e2
    two = lax.broadcasted_iota(jnp.int32, (tr, TOP_K), 1)
    prob_ref[...] = jnp.where(two == 0, 1.0 / denom, e2 / denom)

    chosen = (sel1 | sel2).astype(F32)
    r = lax.broadcasted_iota(jnp.int32, (tr, tr), 0)
    c = lax.broadcasted_iota(jnp.int32, (tr, tr), 1)
    earlier = (c < r).astype(BF16)
    before = jnp.dot(earlier, chosen.astype(BF16), preferred_element_type=F32) + carry_ref[...]
    rank1 = jnp.sum(jnp.where(sel1, before, 0.0), axis=1, keepdims=True).astype(jnp.int32)
    rank2 = jnp.sum(jnp.where(sel2, before, 0.0), axis=1, keepdims=True).astype(jnp.int32)
    code_ref[...] = jnp.where(two == 0, i1 * n_tokens + rank1, i2 * n_tokens + rank2)
    total = carry_ref[...] + jnp.sum(chosen, axis=0, keepdims=True)
    carry_ref[...] = total
    cnt_ref[...] = total.astype(jnp.int32)


def rmsnorm_router(x, g, router_w, router_b, tr=256):
    n, d = x.shape
    return pl.pallas_call(
        functools.partial(_rmsnorm_router_kernel, n_tokens=n),
        grid=(n // tr,),
        in_specs=[pl.BlockSpec((tr, d), lambda i: (i, 0)),
                  pl.BlockSpec((1, d), lambda i: (0, 0)),
                  pl.BlockSpec((d, N_EXPERTS), lambda i: (0, 0)),
                  pl.BlockSpec((1, N_EXPERTS), lambda i: (0, 0))],
        out_specs=[pl.BlockSpec((tr, d // 2), lambda i: (i, 0)),
                   pl.BlockSpec((tr, TOP_K), lambda i: (i, 0)),
                   pl.BlockSpec((tr, TOP_K), lambda i: (i, 0)),
                   pl.BlockSpec((1, N_EXPERTS), lambda i: (0, 0))],
        out_shape=[jax.ShapeDtypeStruct((n, d // 2), jnp.uint32),
                   jax.ShapeDtypeStruct((n, TOP_K), jnp.int32),
                   jax.ShapeDtypeStruct((n, TOP_K), F32),
                   jax.ShapeDtypeStruct((1, N_EXPERTS), jnp.int32)],
        scratch_shapes=[pltpu.VMEM((1, N_EXPERTS), F32)],
        compiler_params=_params("arbitrary"),
        name="rmsnorm_router",
    )(x, g.reshape(1, d), router_w, router_b.reshape(1, N_EXPERTS))


def _row_copy(src_ref, src_row, dst_ref, dst_row, sem, rows=1):
    return pltpu.make_async_copy(src_ref.at[pl.ds(src_row, rows)],
                                 dst_ref.at[pl.ds(dst_row, rows)], sem)


def _chunked_row_copies(n_copies, start_copy, wait_chunk):
    def chunk_body(c, carry):
        def issue(k, carry2):
            start_copy(c * DMA_CHUNK + k)
            return carry2
        lax.fori_loop(0, DMA_CHUNK, issue, 0, unroll=8)

        @pl.when(c > 0)
        def _():
            wait_chunk()
        return carry
    lax.fori_loop(0, n_copies // DMA_CHUNK, chunk_body, 0)
    wait_chunk()


def _dispatch_kernel(code_ref, cnt_ref, src_ref, dst_ref, slot_ref, tile_e_ref, nused_ref,
                     off_ref, sem, *, n_tokens, n_tiles):
    shift = n_tokens.bit_length() - 1
    round_up = lambda v: ((v + MOE_TILE - 1) // MOE_TILE) * MOE_TILE
    off_ref[0] = 0
    for e in range(N_EXPERTS):
        off_ref[e + 1] = off_ref[e] + round_up(cnt_ref[e])
    n_used = off_ref[N_EXPERTS] // MOE_TILE
    nused_ref[0] = n_used

    def tile_body(i, carry):
        row = jnp.minimum(i, n_used - 1) * MOE_TILE
        e = jnp.int32(0)
        for k in range(1, N_EXPERTS):
            e = e + (row >= off_ref[k]).astype(jnp.int32)
        tile_e_ref[i] = e
        return carry
    lax.fori_loop(0, n_tiles, tile_body, 0)

    def start_copy(i):
        code = code_ref[i]
        slot = off_ref[code >> shift] + (code & (n_tokens - 1))
        slot_ref[i] = slot
        _row_copy(src_ref, i >> 1, dst_ref, slot, sem).start()

    _chunked_row_copies(TOP_K * n_tokens, start_copy,
                        lambda: _row_copy(src_ref, 0, dst_ref, 0, sem, DMA_CHUNK).wait())

    for e in range(N_EXPERTS):
        lo = cnt_ref[e]
        hi = round_up(lo)
        base = off_ref[e]

        def pad_start(row, carry):
            _row_copy(src_ref, 0, dst_ref, base + row, sem).start()
            return carry
        lax.fori_loop(lo, hi, pad_start, 0)

        def pad_wait(row, carry):
            _row_copy(src_ref, 0, dst_ref, 0, sem).wait()
            return carry
        lax.fori_loop(lo, hi, pad_wait, 0)


def moe_dispatch(codes, counts, rows):
    n, width = rows.shape
    assert n & (n - 1) == 0 and n % DMA_CHUNK == 0
    n_tiles = (TOP_K * n) // MOE_TILE + N_EXPERTS
    smem = pl.BlockSpec(memory_space=pltpu.SMEM)
    return pl.pallas_call(
        functools.partial(_dispatch_kernel, n_tokens=n, n_tiles=n_tiles),
        grid_spec=pltpu.PrefetchScalarGridSpec(
            num_scalar_prefetch=2, grid=(1,),
            in_specs=[pl.BlockSpec(memory_space=pl.ANY)],
            out_specs=[pl.BlockSpec(memory_space=pl.ANY), smem, smem, smem],
            scratch_shapes=[pltpu.SMEM((N_EXPERTS + 1,), jnp.int32),
                            pltpu.SemaphoreType.DMA(())]),
        out_shape=[jax.ShapeDtypeStruct((n_tiles * MOE_TILE, width), rows.dtype),
                   jax.ShapeDtypeStruct((TOP_K * n,), jnp.int32),
                   jax.ShapeDtypeStruct((n_tiles,), jnp.int32),
                   jax.ShapeDtypeStruct((1,), jnp.int32)],
        compiler_params=_params("arbitrary"),
        name="moe_dispatch",
    )(codes.reshape(TOP_K * n), counts.reshape(N_EXPERTS), rows)


def _return_kernel(slot_ref, src_ref, dst_ref, sem, *, n_tokens):
    def start_copy(i):
        _row_copy(src_ref, slot_ref[i], dst_ref, (i & 1) * n_tokens + (i >> 1), sem).start()

    _chunked_row_copies(TOP_K * n_tokens, start_copy,
                        lambda: _row_copy(src_ref, 0, dst_ref, 0, sem, DMA_CHUNK).wait())


def moe_return(slots, sorted_rows, n):
    width = sorted_rows.shape[1]
    return pl.pallas_call(
        functools.partial(_return_kernel, n_tokens=n),
        grid_spec=pltpu.PrefetchScalarGridSpec(
            num_scalar_prefetch=1, grid=(1,),
            in_specs=[pl.BlockSpec(memory_space=pl.ANY)],
            out_specs=pl.BlockSpec(memory_space=pl.ANY),
            scratch_shapes=[pltpu.SemaphoreType.DMA(())]),
        out_shape=jax.ShapeDtypeStruct((TOP_K * n, width), sorted_rows.dtype),
        compiler_params=_params("arbitrary"),
        name="moe_return",
    )(slots, sorted_rows)


def _moe_up_kernel(tile_e_ref, nused_ref, x_ref, wg_ref, wu_ref, o_ref):
    del tile_e_ref

    @pl.when(pl.program_id(1) < nused_ref[0])
    def _():
        half = x_ref.shape[1]
        hi, lo = _unpack_bf16_pairs(x_ref[...])
        hi = hi.astype(BF16)
        lo = lo.astype(BF16)
        gate = (jnp.dot(hi, wg_ref[:half, :], preferred_element_type=F32)
                + jnp.dot(lo, wg_ref[half:, :], preferred_element_type=F32))
        up = (jnp.dot(hi, wu_ref[:half, :], preferred_element_type=F32)
              + jnp.dot(lo, wu_ref[half:, :], preferred_element_type=F32))
        o_ref[...] = (jax.nn.silu(gate) * up).astype(o_ref.dtype)


def moe_up(xs, tile_e, n_used, w_gate, w_up, tn=512):
    p, half = xs.shape
    n_exp, k, f = w_gate.shape
    n_tiles = p // MOE_TILE
    row_tile = lambda j, i, te, nu: (jnp.minimum(i, nu[0] - 1), 0)
    w_spec = pl.BlockSpec((None, k, tn), lambda j, i, te, nu: (te[i], 0, j))
    return pl.pallas_call(
        _moe_up_kernel,
        grid_spec=pltpu.PrefetchScalarGridSpec(
            num_scalar_prefetch=2, grid=(f // tn, n_tiles),
            in_specs=[pl.BlockSpec((MOE_TILE, half), row_tile), w_spec, w_spec],
            out_specs=pl.BlockSpec((MOE_TILE, tn),
                                   lambda j, i, te, nu: (jnp.minimum(i, nu[0] - 1), j))),
        out_shape=jax.ShapeDtypeStruct((p, f), BF16),
        compiler_params=_params("arbitrary", "arbitrary"),
        name="moe_up",
    )(tile_e, n_used, xs, w_gate, w_up)


def _moe_down_kernel(tile_e_ref, nused_ref, a_ref, w_ref, o_ref):
    del tile_e_ref

    @pl.when(pl.program_id(0) < nused_ref[0])
    def _():
        y = jnp.dot(a_ref[...], w_ref[...], preferred_element_type=F32)
        o_ref[...] = _pack_bf16_pairs(y)


def moe_down(act, tile_e, n_used, w_down):
    p, f = act.shape
    n_exp, _, d = w_down.shape
    row_tile = lambda i, te, nu: (jnp.minimum(i, nu[0] - 1), 0)
    return pl.pallas_call(
        _moe_down_kernel,
        grid_spec=pltpu.PrefetchScalarGridSpec(
            num_scalar_prefetch=2, grid=(p // MOE_TILE,),
            in_specs=[pl.BlockSpec((MOE_TILE, f), row_tile),
                      pl.BlockSpec((None, f, d), lambda i, te, nu: (te[i], 0, 0))],
            out_specs=pl.BlockSpec((MOE_TILE, d // 2), row_tile)),
        out_shape=jax.ShapeDtypeStruct((p, d // 2), jnp.uint32),
        compiler_params=_params("arbitrary"),
        name="moe_down",
    )(tile_e, n_used, act, w_down)


def _moe_combine_kernel(h_ref, y1_ref, y2_ref, p_ref, o_ref):
    half = y1_ref.shape[1]
    p1 = p_ref[:, 0:1]
    p2 = p_ref[:, 1:2]
    hi1, lo1 = _unpack_bf16_pairs(y1_ref[...])
    hi2, lo2 = _unpack_bf16_pairs(y2_ref[...])
    o_ref[:, :half] = h_ref[:, :half] + (p1 * hi1 + p2 * hi2)
    o_ref[:, half:] = h_ref[:, half:] + (p1 * lo1 + p2 * lo2)


def moe_combine(h, y12, probs, tr=256):
    n, d = h.shape
    tiles = n // tr
    return pl.pallas_call(
        _moe_combine_kernel,
        grid=(tiles,),
        in_specs=[pl.BlockSpec((tr, d), lambda i: (i, 0)),
                  pl.BlockSpec((tr, d // 2), lambda i: (i, 0)),
                  pl.BlockSpec((tr, d // 2), lambda i: (tiles + i, 0)),
                  pl.BlockSpec((tr, TOP_K), lambda i: (i, 0))],
        out_specs=pl.BlockSpec((tr, d), lambda i: (i, 0)),
        out_shape=jax.ShapeDtypeStruct((n, d), F32),
        input_output_aliases={0: 0},
        compiler_params=_params("arbitrary"),
        name="moe_combine",
    )(h, y12, y12, probs)


def moe_ffn(h, norm_g, router_w, router_b, w_gate, w_up, w_down):
    n = h.shape[0]
    rows, codes, probs, counts = rmsnorm_router(h, norm_g, router_w, router_b)
    xs, slots, tile_e, n_used = moe_dispatch(codes, counts, rows)
    act = moe_up(xs, tile_e, n_used, w_gate, w_up)
    ys = moe_down(act, tile_e, n_used, w_down)
    y12 = moe_return(slots, ys, n)
    return moe_combine(h, y12, probs)


def _matmul_kernel(x_ref, w_ref, o_ref):
    o_ref[...] = jnp.dot(x_ref[...], w_ref[...],
                         preferred_element_type=F32).astype(o_ref.dtype)


def matmul_cols(x, w, n_cols, out_dtype, tm=1024, tn=1024):
    m, k = x.shape
    return pl.pallas_call(
        _matmul_kernel,
        grid=(n_cols // tn, m // tm),
        in_specs=[pl.BlockSpec((tm, k), lambda j, i: (i, 0)),
                  pl.BlockSpec((k, tn), lambda j, i: (0, j))],
        out_specs=pl.BlockSpec((tm, tn), lambda j, i: (i, j)),
        out_shape=jax.ShapeDtypeStruct((m, n_cols), out_dtype),
        compiler_params=_params("arbitrary", "arbitrary"),
    )(x, w)


def _matmul_residual_kernel(x_ref, w_ref, r_ref, o_ref):
    o_ref[...] = r_ref[...] + jnp.dot(x_ref[...], w_ref[...], preferred_element_type=F32)


def matmul_residual(x, w, res, tm=1024, tn=512):
    m, k = x.shape
    n = w.shape[1]
    return pl.pallas_call(
        _matmul_residual_kernel,
        grid=(n // tn, m // tm),
        in_specs=[pl.BlockSpec((tm, k), lambda j, i: (i, 0)),
                  pl.BlockSpec((k, tn), lambda j, i: (0, j)),
                  pl.BlockSpec((tm, tn), lambda j, i: (i, j))],
        out_specs=pl.BlockSpec((tm, tn), lambda j, i: (i, j)),
        out_shape=jax.ShapeDtypeStruct((m, n), F32),
        input_output_aliases={2: 0},
        compiler_params=_params("arbitrary", "arbitrary"),
    )(x, w, res)


def _gelu(x):
    return 0.5 * x * (1.0 + lax.erf(x * (2.0 ** -0.5)))


def _shift_rows(x, k):
    return pltpu.roll(x, k, axis=0)


def _mixer_kernel(p_ref, poolw_ref, pscale_ref, sgunorm_ref, ws_ref, sb_ref, convw_ref,
                  y_ref, halo_ref, *, ts):
    w = BRANCH_WIDTH
    s_idx = pl.program_id(1)

    @pl.when(s_idx == 0)
    def _():
        halo_ref[...] = jnp.zeros_like(halo_ref)

    a = p_ref[:, 0:w].astype(F32)
    ext_a = jnp.concatenate([halo_ref[:, 0:w], a], axis=0)
    pos = s_idx * ts + lax.broadcasted_iota(jnp.int32, (ts, 1), 0)
    for g, win in enumerate(POOL_WINDOWS):
        cols = slice(g * POOL_GROUP_DIM, (g + 1) * POOL_GROUP_DIM)
        acc = ext_a[:, cols]
        step = 1
        while step < win:
            acc = acc + _shift_rows(acc, step)
            step *= 2
        cnt = jnp.minimum(pos + 1, win).astype(F32)
        pooled = acc[HALO_ROWS:, :] / cnt - a[:, cols]
        ya = jnp.dot(pooled.astype(BF16), poolw_ref[g], preferred_element_type=F32)
        y_ref[:, cols] = (ya * pscale_ref[:, cols]).astype(y_ref.dtype)

    u = _gelu(p_ref[:, w:2 * w].astype(F32))
    v = _rms(_gelu(p_ref[:, 2 * w:3 * w].astype(F32)), sgunorm_ref[...]).astype(BF16)
    row = lax.broadcasted_iota(jnp.int32, (SGU_CHUNK, SGU_CHUNK), 0)
    col = lax.broadcasted_iota(jnp.int32, (SGU_CHUNK, SGU_CHUNK), 1)
    for g in range(SGU_HEADS):
        cols = slice(g * SGU_HEAD_DIM, (g + 1) * SGU_HEAD_DIM)
        w_causal = jnp.where(row >= col, ws_ref[g], 0.0).astype(BF16)
        bias = sb_ref[:, g:g + 1]
        for c in range(ts // SGU_CHUNK):
            rows = slice(c * SGU_CHUNK, (c + 1) * SGU_CHUNK)
            vm = jnp.dot(w_causal, v[rows, cols], preferred_element_type=F32) + bias
            y_ref[rows, w + g * SGU_HEAD_DIM:w + (g + 1) * SGU_HEAD_DIM] = (
                u[rows, cols] * vm).astype(y_ref.dtype)

    hc = p_ref[:, 5 * w:6 * w].astype(F32) * p_ref[:, 3 * w:4 * w].astype(F32)
    ext_h = jnp.concatenate([halo_ref[:, w:2 * w], hc], axis=0)
    yc = (convw_ref[2:3, :] * ext_h
          + convw_ref[1:2, :] * _shift_rows(ext_h, 1)
          + convw_ref[0:1, :] * _shift_rows(ext_h, 2))
    y_ref[:, 2 * w:3 * w] = (p_ref[:, 4 * w:5 * w].astype(F32)
                             * yc[HALO_ROWS:, :]).astype(y_ref.dtype)

    halo_ref[:, 0:w] = a[ts - HALO_ROWS:, :]
    halo_ref[:, w:2 * w] = hc[ts - HALO_ROWS:, :]


def token_mixers(p6, pool_w, pool_scale, sgu_norm, sgu_ws, sgu_b, conv_w, bsz, seq, ts=256):
    n = p6.shape[0]
    w = BRANCH_WIDTH
    tiles = seq // ts
    row_block = lambda b, s: (b * tiles + s, 0)
    const2 = lambda b, s: (0, 0)
    const3 = lambda b, s: (0, 0, 0)
    return pl.pallas_call(
        functools.partial(_mixer_kernel, ts=ts),
        grid=(bsz, tiles),
        in_specs=[pl.BlockSpec((ts, MIX_COLS), row_block),
                  pl.BlockSpec((POOL_GROUPS, POOL_GROUP_DIM, POOL_GROUP_DIM), const3),
                  pl.BlockSpec((1, w), const2),
                  pl.BlockSpec((1, w), const2),
                  pl.BlockSpec((SGU_HEADS, SGU_CHUNK, SGU_CHUNK), const3),
                  pl.BlockSpec((SGU_CHUNK, SGU_HEADS), const2),
                  pl.BlockSpec((CONV_WIDTH, w), const2)],
        out_specs=pl.BlockSpec((ts, N_BRANCHES * w), row_block),
        out_shape=jax.ShapeDtypeStruct((n, N_BRANCHES * w), BF16),
        scratch_shapes=[pltpu.VMEM((HALO_ROWS, 2 * w), F32)],
        compiler_params=_params("arbitrary", "arbitrary"),
    )(p6, pool_w.astype(BF16), pool_scale.reshape(1, w), sgu_norm.reshape(1, w),
      sgu_ws, sgu_b.T, conv_w)


def _gate_merge_kernel(xn_ref, y_ref, wga_ref, wgb_ref, wgc_ref, wb_ref, o_ref):
    w = BRANCH_WIDTH
    xn = xn_ref[...]
    acc = None
    for n, wg_ref in enumerate((wga_ref, wgb_ref, wgc_ref)):
        gate = jnp.dot(xn, wg_ref[...], preferred_element_type=F32)
        branch = jnp.dot(y_ref[:, n * w:(n + 1) * w], wb_ref[n], preferred_element_type=F32)
        term = jax.nn.sigmoid(gate) * branch
        acc = term if acc is None else acc + term
    o_ref[...] = acc.astype(o_ref.dtype)


def gate_merge(xn, y, w_in, w_branch, tm=512, tn=512):
    m, d = xn.shape
    w = BRANCH_WIDTH
    gate_col0 = MIX_COLS // tn
    gate_stride = D_MODEL // tn

    def gate_spec(n):
        return pl.BlockSpec((d, tn), lambda j, i: (0, gate_col0 + n * gate_stride + j))

    return pl.pallas_call(
        _gate_merge_kernel,
        grid=(D_MODEL // tn, m // tm),
        in_specs=[pl.BlockSpec((tm, d), lambda j, i: (i, 0)),
                  pl.BlockSpec((tm, N_BRANCHES * w), lambda j, i: (i, 0)),
                  gate_spec(0), gate_spec(1), gate_spec(2),
                  pl.BlockSpec((N_BRANCHES, w, tn), lambda j, i: (0, 0, j))],
        out_specs=pl.BlockSpec((tm, tn), lambda j, i: (i, j)),
        out_shape=jax.ShapeDtypeStruct((m, D_MODEL), BF16),
        compiler_params=_params("arbitrary", "arbitrary"),
    )(xn, y, w_in, w_in, w_in, w_branch)


def _swiglu_up_kernel(x_ref, wg_ref, wu_ref, o_ref):
    x = x_ref[...]
    gate = jnp.dot(x, wg_ref[...], preferred_element_type=F32)
    up = jnp.dot(x, wu_ref[...], preferred_element_type=F32)
    o_ref[...] = (jax.nn.silu(gate) * up).astype(o_ref.dtype)


def swiglu_up(x, w_gate, w_up, tm=1024, tn=512):
    m, k = x.shape
    n = w_gate.shape[1]
    w_spec = pl.BlockSpec((k, tn), lambda j, i: (0, j))
    return pl.pallas_call(
        _swiglu_up_kernel,
        grid=(n // tn, m // tm),
        in_specs=[pl.BlockSpec((tm, k), lambda j, i: (i, 0)), w_spec, w_spec],
        out_specs=pl.BlockSpec((tm, tn), lambda j, i: (i, j)),
        out_shape=jax.ShapeDtypeStruct((m, n), BF16),
        compiler_params=_params("arbitrary", "arbitrary"),
    )(x, w_gate, w_up)


def kernel(x, norm_mix, w_in, pool_w, pool_scale, sgu_norm, sgu_ws, sgu_b, conv_w, w_branch, w_out, norm_ffn, dense_w_gate, dense_w_up, dense_w_down, router_w, router_b, moe_w_gate, moe_w_up, moe_w_down, norm_final):
    bsz, seq, d = x.shape
    h = x.reshape(bsz * seq, d)
    for layer in range(DEPTH):
        w_in_l = w_in[layer].astype(BF16)
        xn = rmsnorm(h, norm_mix[layer], BF16)
        p6 = matmul_cols(xn, w_in_l, MIX_COLS, BF16)
        y = token_mixers(p6, pool_w[layer], pool_scale[layer], sgu_norm[layer], sgu_ws[layer],
                         sgu_b[layer], conv_w[layer], bsz, seq)
        merged = gate_merge(xn, y, w_in_l, w_branch[layer].astype(BF16))
        h = matmul_residual(merged, w_out[layer].astype(BF16), h)
        j = layer // 2
        if layer % 2 == 0:
            hn = rmsnorm(h, norm_ffn[layer], BF16)
            act = swiglu_up(hn, dense_w_gate[j].astype(BF16), dense_w_up[j].astype(BF16))
            h = matmul_residual(act, dense_w_down[j].astype(BF16), h)
        else:
            h = moe_ffn(h, norm_ffn[layer], router_w[j], router_b[j],
                        moe_w_gate[j].astype(BF16), moe_w_up[j].astype(BF16),
                        moe_w_down[j].astype(BF16))
    return rmsnorm(h, norm_final, F32).reshape(bsz, seq, d)
```

```python
import functools

import jax
import jax.numpy as jnp
from jax import lax
from jax.experimental import pallas as pl
from jax.experimental.pallas import tpu as pltpu

D_MODEL = 4096
DEPTH = 4
BRANCH_WIDTH = D_MODEL // 4
N_BRANCHES = 3
POOL_WINDOWS = (2, 4, 8, 16)
POOL_GROUPS = 4
POOL_GROUP_DIM = BRANCH_WIDTH // POOL_GROUPS
SGU_CHUNK = 128
SGU_HEADS = 4
SGU_HEAD_DIM = BRANCH_WIDTH // SGU_HEADS
CONV_WIDTH = 3
MIX_COLS = 6 * BRANCH_WIDTH
D_FF = (3 * D_MODEL) // 2
N_EXPERTS = 8
TOP_K = 2
D_FF_EXPERT = D_FF // 4
EPS = 1e-6

V7X_VMEM_LIMIT_BYTES = 58 * 1024 * 1024
HALO_ROWS = 16

BF16 = jnp.bfloat16
F32 = jnp.float32


def _params(*semantics):
    return pltpu.CompilerParams(dimension_semantics=semantics,
                                vmem_limit_bytes=V7X_VMEM_LIMIT_BYTES)


def _rms(x, g):
    ms = jnp.mean(x * x, axis=-1, keepdims=True)
    return x * lax.rsqrt(ms + EPS) * g


def _rmsnorm_kernel(x_ref, g_ref, o_ref):
    o_ref[...] = _rms(x_ref[...], g_ref[...]).astype(o_ref.dtype)


def rmsnorm(x, g, out_dtype, tr=256):
    n, d = x.shape
    return pl.pallas_call(
        _rmsnorm_kernel,
        grid=(n // tr,),
        in_specs=[pl.BlockSpec((tr, d), lambda i: (i, 0)),
                  pl.BlockSpec((1, d), lambda i: (0, 0))],
        out_specs=pl.BlockSpec((tr, d), lambda i: (i, 0)),
        out_shape=jax.ShapeDtypeStruct((n, d), out_dtype),
        compiler_params=_params("arbitrary"),
    )(x, g.reshape(1, d))


MOE_TILE = 512
HIGH_HALF = 0xFFFF0000


def _pack_bf16_pairs(x):
    half = x.shape[1] // 2
    hi = lax.bitcast_convert_type(x[:, :half].astype(BF16).astype(F32), jnp.uint32)
    lo = lax.bitcast_convert_type(x[:, half:].astype(BF16).astype(F32), jnp.uint32)
    return (hi & jnp.uint32(HIGH_HALF)) | (lo >> 16)


def _unpack_bf16_pairs(p):
    hi = lax.bitcast_convert_type(p & jnp.uint32(HIGH_HALF), F32)
    lo = lax.bitcast_convert_type(p << 16, F32)
    return hi, lo


def _rmsnorm_router_kernel(x_ref, g_ref, rw_ref, rb_ref, o_ref, code_ref, prob_ref, cnt_ref,
                           carry_ref, *, n_tokens):
    i = pl.program_id(0)
    tr = x_ref.shape[0]

    @pl.when(i == 0)
    def _():
        carry_ref[...] = jnp.zeros_like(carry_ref)

    hn = _rms(x_ref[...], g_ref[...])
    o_ref[...] = _pack_bf16_pairs(hn)
    logits = jnp.dot(hn, rw_ref[...], precision=lax.Precision.HIGHEST,
                     preferred_element_type=F32) + rb_ref[...]
    idx = lax.broadcasted_iota(jnp.int32, logits.shape, 1)
    m1 = jnp.max(logits, axis=1, keepdims=True)
    i1 = jnp.min(jnp.where(logits == m1, idx, N_EXPERTS), axis=1, keepdims=True)
    sel1 = idx == i1
    rest = jnp.where(sel1, -jnp.inf, logits)
    m2 = jnp.max(rest, axis=1, keepdims=True)
    i2 = jnp.min(jnp.where((rest == m2) & jnp.logical_not(sel1), idx, N_EXPERTS),
                 axis=1, keepdims=True)
    sel2 = idx == i2
    e2 = jnp.exp(m2 - m1)
    denom = 1.0 + e2
    two = lax.broadcasted_iota(jnp.int32, (tr, TOP_K), 1)
    prob_ref[...] = jnp.where(two == 0, 1.0 / denom, e2 / denom)

    chosen = (sel1 | sel2).astype(F32)
    r = lax.broadcasted_iota(jnp.int32, (tr, tr), 0)
    c = lax.broadcasted_iota(jnp.int32, (tr, tr), 1)
    earlier = (c < r).astype(BF16)
    before = jnp.dot(earlier, chosen.astype(BF16), preferred_element_type=F32) + carry_ref[...]
    rank1 = jnp.sum(jnp.where(sel1, before, 0.0), axis=1, keepdims=True).astype(jnp.int32)
    rank2 = jnp.sum(jnp.where(sel2, before, 0.0), axis=1, keepdims=True).astype(jnp.int32)
    code_ref[...] = jnp.where(two == 0, i1 * n_tokens + rank1, i2 * n_tokens + rank2)
    total = carry_ref[...] + jnp.sum(chosen, axis=0, keepdims=True)
    carry_ref[...] = total
    cnt_ref[...] = total.astype(jnp.int32)


def rmsnorm_router(x, g, router_w, router_b, tr=256):
    n, d = x.shape
    return pl.pallas_call(
        functools.partial(_rmsnorm_router_kernel, n_tokens=n),
        grid=(n // tr,),
        in_specs=[pl.BlockSpec((tr, d), lambda i: (i, 0)),
                  pl.BlockSpec((1, d), lambda i: (0, 0)),
                  pl.BlockSpec((d, N_EXPERTS), lambda i: (0, 0)),
                  pl.BlockSpec((1, N_EXPERTS), lambda i: (0, 0))],
        out_specs=[pl.BlockSpec((tr, d // 2), lambda i: (i, 0)),
                   pl.BlockSpec((tr, TOP_K), lambda i: (i, 0)),
                   pl.BlockSpec((tr, TOP_K), lambda i: (i, 0)),
                   pl.BlockSpec((1, N_EXPERTS), lambda i: (0, 0))],
        out_shape=[jax.ShapeDtypeStruct((n, d // 2), jnp.uint32),
                   jax.ShapeDtypeStruct((n, TOP_K), jnp.int32),
                   jax.ShapeDtypeStruct((n, TOP_K), F32),
                   jax.ShapeDtypeStruct((1, N_EXPERTS), jnp.int32)],
        scratch_shapes=[pltpu.VMEM((1, N_EXPERTS), F32)],
        compiler_params=_params("arbitrary"),
        name="rmsnorm_router",
    )(x, g.reshape(1, d), router_w, router_b.reshape(1, N_EXPERTS))


def _row_copy(src_ref, src_row, dst_ref, dst_row, sem, rows=1):
    return pltpu.make_async_copy(src_ref.at[pl.ds(src_row, rows)],
                                 dst_ref.at[pl.ds(dst_row, rows)], sem)


def _dispatch_kernel(code_ref, cnt_ref, src_ref, dst_ref, slot_ref, tile_e_ref, nused_ref,
                     off_ref, sem, *, n_tokens, n_tiles):
    step = pl.program_id(0)
    tr = src_ref.shape[0]
    shift = n_tokens.bit_length() - 1
    round_up = lambda v: ((v + MOE_TILE - 1) // MOE_TILE) * MOE_TILE

    @pl.when(step == 0)
    def _():
        off_ref[0] = 0
        for e in range(N_EXPERTS):
            off_ref[e + 1] = off_ref[e] + round_up(cnt_ref[e])
        n_used = off_ref[N_EXPERTS] // MOE_TILE
        nused_ref[0] = n_used

        def tile_body(i, carry):
            row = jnp.minimum(i, n_used - 1) * MOE_TILE
            e = jnp.int32(0)
            for k in range(1, N_EXPERTS):
                e = e + (row >= off_ref[k]).astype(jnp.int32)
            tile_e_ref[i] = e
            return carry
        lax.fori_loop(0, n_tiles, tile_body, 0)

    def issue(k, carry):
        for kk in range(TOP_K):
            i = TOP_K * (step * tr + k) + kk
            code = code_ref[i]
            slot = off_ref[code >> shift] + (code & (n_tokens - 1))
            slot_ref[i] = slot
            _row_copy(src_ref, k, dst_ref, slot, sem).start()
        return carry
    lax.fori_loop(0, tr, issue, 0, unroll=4)
    _row_copy(dst_ref, 0, dst_ref, 0, sem, TOP_K * tr).wait()

    @pl.when(step == pl.num_programs(0) - 1)
    def _():
        def fill(lo, hi):
            def pad_start(row, carry):
                _row_copy(src_ref, 0, dst_ref, row, sem).start()
                return carry
            lax.fori_loop(lo, hi, pad_start, 0)

            def pad_wait(row, carry):
                _row_copy(src_ref, 0, dst_ref, 0, sem).wait()
                return carry
            lax.fori_loop(lo, hi, pad_wait, 0)

        for e in range(N_EXPERTS):
            fill(off_ref[e] + cnt_ref[e], off_ref[e + 1])
        fill(off_ref[N_EXPERTS], n_tiles * MOE_TILE)


def moe_dispatch(codes, counts, rows, tr=256):
    n, width = rows.shape
    assert n & (n - 1) == 0
    n_tiles = (TOP_K * n) // MOE_TILE + N_EXPERTS
    smem = pl.BlockSpec(memory_space=pltpu.SMEM)
    return pl.pallas_call(
        functools.partial(_dispatch_kernel, n_tokens=n, n_tiles=n_tiles),
        grid_spec=pltpu.PrefetchScalarGridSpec(
            num_scalar_prefetch=2, grid=(n // tr,),
            in_specs=[pl.BlockSpec((tr, width), lambda i, code, cnt: (i, 0))],
            out_specs=[pl.BlockSpec(memory_space=pl.ANY), smem, smem, smem],
            scratch_shapes=[pltpu.SMEM((N_EXPERTS + 1,), jnp.int32),
                            pltpu.SemaphoreType.DMA(())]),
        out_shape=[jax.ShapeDtypeStruct((n_tiles * MOE_TILE, width), rows.dtype),
                   jax.ShapeDtypeStruct((TOP_K * n,), jnp.int32),
                   jax.ShapeDtypeStruct((n_tiles,), jnp.int32),
                   jax.ShapeDtypeStruct((1,), jnp.int32)],
        compiler_params=_params("arbitrary"),
        name="moe_dispatch",
    )(codes.reshape(TOP_K * n), counts.reshape(N_EXPERTS), rows)


def _moe_up_kernel(tile_e_ref, nused_ref, x_ref, wg_ref, wu_ref, o_ref):
    del tile_e_ref

    @pl.when(pl.program_id(1) < nused_ref[0])
    def _():
        half = x_ref.shape[1]
        hi, lo = _unpack_bf16_pairs(x_ref[...])
        hi = hi.astype(BF16)
        lo = lo.astype(BF16)
        gate = (jnp.dot(hi, wg_ref[:half, :], preferred_element_type=F32)
                + jnp.dot(lo, wg_ref[half:, :], preferred_element_type=F32))
        up = (jnp.dot(hi, wu_ref[:half, :], preferred_element_type=F32)
              + jnp.dot(lo, wu_ref[half:, :], preferred_element_type=F32))
        o_ref[...] = (jax.nn.silu(gate) * up).astype(o_ref.dtype)

    @pl.when(pl.program_id(1) >= nused_ref[0])
    def _():
        o_ref[...] = jnp.zeros_like(o_ref)


def moe_up(xs, tile_e, n_used, w_gate, w_up, tn=512):
    p, half = xs.shape
    n_exp, k, f = w_gate.shape
    n_tiles = p // MOE_TILE
    row_tile = lambda j, i, te, nu: (jnp.minimum(i, nu[0] - 1), 0)
    w_spec = pl.BlockSpec((None, k, tn), lambda j, i, te, nu: (te[i], 0, j))
    return pl.pallas_call(
        _moe_up_kernel,
        grid_spec=pltpu.PrefetchScalarGridSpec(
            num_scalar_prefetch=2, grid=(f // tn, n_tiles),
            in_specs=[pl.BlockSpec((MOE_TILE, half), row_tile), w_spec, w_spec],
            out_specs=pl.BlockSpec((MOE_TILE, tn), lambda j, i, te, nu: (i, j))),
        out_shape=jax.ShapeDtypeStruct((p, f), BF16),
        compiler_params=_params("arbitrary", "arbitrary"),
        name="moe_up",
    )(tile_e, n_used, xs, w_gate, w_up)


def _moe_down_kernel(tile_e_ref, nused_ref, a_ref, w_ref, o_ref):
    del tile_e_ref

    @pl.when(pl.program_id(0) < nused_ref[0])
    def _():
        y = jnp.dot(a_ref[...], w_ref[...], preferred_element_type=F32)
        o_ref[...] = _pack_bf16_pairs(y)

    @pl.when(pl.program_id(0) >= nused_ref[0])
    def _():
        o_ref[...] = jnp.zeros_like(o_ref)


def moe_down(act, tile_e, n_used, w_down):
    p, f = act.shape
    n_exp, _, d = w_down.shape
    row_tile = lambda i, te, nu: (jnp.minimum(i, nu[0] - 1), 0)
    return pl.pallas_call(
        _moe_down_kernel,
        grid_spec=pltpu.PrefetchScalarGridSpec(
            num_scalar_prefetch=2, grid=(p // MOE_TILE,),
            in_specs=[pl.BlockSpec((MOE_TILE, f), row_tile),
                      pl.BlockSpec((None, f, d), lambda i, te, nu: (te[i], 0, 0))],
            out_specs=pl.BlockSpec((MOE_TILE, d // 2), lambda i, te, nu: (i, 0))),
        out_shape=jax.ShapeDtypeStruct((p, d // 2), jnp.uint32),
        compiler_params=_params("arbitrary"),
        name="moe_down",
    )(tile_e, n_used, act, w_down)


def _moe_combine_kernel(slot_ref, h_ref, p_ref, ys_ref, o_ref, buf_ref, sems):
    i = pl.program_id(0)
    tr = h_ref.shape[0]
    half = ys_ref.shape[1]

    def start_gather(tile, b):
        def issue(k, carry):
            for kk in range(TOP_K):
                slot = slot_ref[TOP_K * (tile * tr + k) + kk]
                _row_copy(ys_ref, slot, buf_ref.at[b, kk], k, sems.at[b]).start()
            return carry
        lax.fori_loop(0, tr, issue, 0, unroll=4)

    @pl.when(i == 0)
    def _():
        start_gather(0, 0)

    @pl.when(i + 1 < pl.num_programs(0))
    def _():
        start_gather(i + 1, (i + 1) % 2)

    b = i % 2
    for kk in range(TOP_K):
        _row_copy(ys_ref, 0, buf_ref.at[b, kk], 0, sems.at[b], tr).wait()

    p1 = p_ref[:, 0:1]
    p2 = p_ref[:, 1:2]
    hi1, lo1 = _unpack_bf16_pairs(buf_ref[b, 0])
    hi2, lo2 = _unpack_bf16_pairs(buf_ref[b, 1])
    o_ref[:, :half] = h_ref[:, :half] + (p1 * hi1 + p2 * hi2)
    o_ref[:, half:] = h_ref[:, half:] + (p1 * lo1 + p2 * lo2)


def moe_combine(h, slots, ys, probs, tr=256):
    n, d = h.shape
    width = ys.shape[1]
    return pl.pallas_call(
        _moe_combine_kernel,
        grid_spec=pltpu.PrefetchScalarGridSpec(
            num_scalar_prefetch=1, grid=(n // tr,),
            in_specs=[pl.BlockSpec((tr, d), lambda i, s: (i, 0)),
                      pl.BlockSpec((tr, TOP_K), lambda i, s: (i, 0)),
                      pl.BlockSpec(memory_space=pl.ANY)],
            out_specs=pl.BlockSpec((tr, d), lambda i, s: (i, 0)),
            scratch_shapes=[pltpu.VMEM((2, TOP_K, tr, width), ys.dtype),
                            pltpu.SemaphoreType.DMA((2,))]),
        out_shape=jax.ShapeDtypeStruct((n, d), F32),
        input_output_aliases={1: 0},
        compiler_params=_params("arbitrary"),
        name="moe_combine",
    )(slots, h, probs, ys)


def moe_ffn(h, norm_g, router_w, router_b, w_gate, w_up, w_down):
    rows, codes, probs, counts = rmsnorm_router(h, norm_g, router_w, router_b)
    xs, slots, tile_e, n_used = moe_dispatch(codes, counts, rows)
    act = moe_up(xs, tile_e, n_used, w_gate, w_up)
    ys = moe_down(act, tile_e, n_used, w_down)
    return moe_combine(h, slots, ys, probs)


def _matmul_kernel(x_ref, w_ref, o_ref):
    o_ref[...] = jnp.dot(x_ref[...], w_ref[...],
                         preferred_element_type=F32).astype(o_ref.dtype)


def matmul_cols(x, w, n_cols, out_dtype, tm=1024, tn=1024):
    m, k = x.shape
    return pl.pallas_call(
        _matmul_kernel,
        grid=(n_cols // tn, m // tm),
        in_specs=[pl.BlockSpec((tm, k), lambda j, i: (i, 0)),
                  pl.BlockSpec((k, tn), lambda j, i: (0, j))],
        out_specs=pl.BlockSpec((tm, tn), lambda j, i: (i, j)),
        out_shape=jax.ShapeDtypeStruct((m, n_cols), out_dtype),
        compiler_params=_params("arbitrary", "arbitrary"),
    )(x, w)


def _matmul_residual_kernel(x_ref, w_ref, r_ref, o_ref):
    o_ref[...] = r_ref[...] + jnp.dot(x_ref[...], w_ref[...], preferred_element_type=F32)


def matmul_residual(x, w, res, tm=1024, tn=512):
    m, k = x.shape
    n = w.shape[1]
    return pl.pallas_call(
        _matmul_residual_kernel,
        grid=(n // tn, m // tm),
        in_specs=[pl.BlockSpec((tm, k), lambda j, i: (i, 0)),
                  pl.BlockSpec((k, tn), lambda j, i: (0, j)),
                  pl.BlockSpec((tm, tn), lambda j, i: (i, j))],
        out_specs=pl.BlockSpec((tm, tn), lambda j, i: (i, j)),
        out_shape=jax.ShapeDtypeStruct((m, n), F32),
        input_output_aliases={2: 0},
        compiler_params=_params("arbitrary", "arbitrary"),
    )(x, w, res)


def _gelu(x):
    return 0.5 * x * (1.0 + lax.erf(x * (2.0 ** -0.5)))


def _shift_rows(x, k):
    return pltpu.roll(x, k, axis=0)


def _mixer_kernel(p_ref, poolw_ref, pscale_ref, sgunorm_ref, ws_ref, sb_ref, convw_ref,
                  y_ref, halo_ref, *, ts):
    w = BRANCH_WIDTH
    s_idx = pl.program_id(1)

    @pl.when(s_idx == 0)
    def _():
        halo_ref[...] = jnp.zeros_like(halo_ref)

    a = p_ref[:, 0:w].astype(F32)
    ext_a = jnp.concatenate([halo_ref[:, 0:w], a], axis=0)
    pos = s_idx * ts + lax.broadcasted_iota(jnp.int32, (ts, 1), 0)
    for g, win in enumerate(POOL_WINDOWS):
        cols = slice(g * POOL_GROUP_DIM, (g + 1) * POOL_GROUP_DIM)
        acc = ext_a[:, cols]
        step = 1
        while step < win:
            acc = acc + _shift_rows(acc, step)
            step *= 2
        cnt = jnp.minimum(pos + 1, win).astype(F32)
        pooled = acc[HALO_ROWS:, :] / cnt - a[:, cols]
        ya = jnp.dot(pooled.astype(BF16), poolw_ref[g], preferred_element_type=F32)
        y_ref[:, cols] = (ya * pscale_ref[:, cols]).astype(y_ref.dtype)

    u = _gelu(p_ref[:, w:2 * w].astype(F32))
    v = _rms(_gelu(p_ref[:, 2 * w:3 * w].astype(F32)), sgunorm_ref[...]).astype(BF16)
    row = lax.broadcasted_iota(jnp.int32, (SGU_CHUNK, SGU_CHUNK), 0)
    col = lax.broadcasted_iota(jnp.int32, (SGU_CHUNK, SGU_CHUNK), 1)
    for g in range(SGU_HEADS):
        cols = slice(g * SGU_HEAD_DIM, (g + 1) * SGU_HEAD_DIM)
        w_causal = jnp.where(row >= col, ws_ref[g], 0.0).astype(BF16)
        bias = sb_ref[:, g:g + 1]
        for c in range(ts // SGU_CHUNK):
            rows = slice(c * SGU_CHUNK, (c + 1) * SGU_CHUNK)
            vm = jnp.dot(w_causal, v[rows, cols], preferred_element_type=F32) + bias
            y_ref[rows, w + g * SGU_HEAD_DIM:w + (g + 1) * SGU_HEAD_DIM] = (
                u[rows, cols] * vm).astype(y_ref.dtype)

    hc = p_ref[:, 5 * w:6 * w].astype(F32) * p_ref[:, 3 * w:4 * w].astype(F32)
    ext_h = jnp.concatenate([halo_ref[:, w:2 * w], hc], axis=0)
    yc = (convw_ref[2:3, :] * ext_h
          + convw_ref[1:2, :] * _shift_rows(ext_h, 1)
          + convw_ref[0:1, :] * _shift_rows(ext_h, 2))
    y_ref[:, 2 * w:3 * w] = (p_ref[:, 4 * w:5 * w].astype(F32)
                             * yc[HALO_ROWS:, :]).astype(y_ref.dtype)

    halo_ref[:, 0:w] = a[ts - HALO_ROWS:, :]
    halo_ref[:, w:2 * w] = hc[ts - HALO_ROWS:, :]


def token_mixers(p6, pool_w, pool_scale, sgu_norm, sgu_ws, sgu_b, conv_w, bsz, seq, ts=256):
    n = p6.shape[0]
    w = BRANCH_WIDTH
    tiles = seq // ts
    row_block = lambda b, s: (b * tiles + s, 0)
    const2 = lambda b, s: (0, 0)
    const3 = lambda b, s: (0, 0, 0)
    return pl.pallas_call(
        functools.partial(_mixer_kernel, ts=ts),
        grid=(bsz, tiles),
        in_specs=[pl.BlockSpec((ts, MIX_COLS), row_block),
                  pl.BlockSpec((POOL_GROUPS, POOL_GROUP_DIM, POOL_GROUP_DIM), const3),
                  pl.BlockSpec((1, w), const2),
                  pl.BlockSpec((1, w), const2),
                  pl.BlockSpec((SGU_HEADS, SGU_CHUNK, SGU_CHUNK), const3),
                  pl.BlockSpec((SGU_CHUNK, SGU_HEADS), const2),
                  pl.BlockSpec((CONV_WIDTH, w), const2)],
        out_specs=pl.BlockSpec((ts, N_BRANCHES * w), row_block),
        out_shape=jax.ShapeDtypeStruct((n, N_BRANCHES * w), BF16),
        scratch_shapes=[pltpu.VMEM((HALO_ROWS, 2 * w), F32)],
        compiler_params=_params("arbitrary", "arbitrary"),
    )(p6, pool_w.astype(BF16), pool_scale.reshape(1, w), sgu_norm.reshape(1, w),
      sgu_ws, sgu_b.T, conv_w)


def _gate_merge_kernel(xn_ref, y_ref, wga_ref, wgb_ref, wgc_ref, wb_ref, o_ref):
    w = BRANCH_WIDTH
    xn = xn_ref[...]
    acc = None
    for n, wg_ref in enumerate((wga_ref, wgb_ref, wgc_ref)):
        gate = jnp.dot(xn, wg_ref[...], preferred_element_type=F32)
        branch = jnp.dot(y_ref[:, n * w:(n + 1) * w], wb_ref[n], preferred_element_type=F32)
        term = jax.nn.sigmoid(gate) * branch
        acc = term if acc is None else acc + term
    o_ref[...] = acc.astype(o_ref.dtype)


def gate_merge(xn, y, w_in, w_branch, tm=512, tn=512):
    m, d = xn.shape
    w = BRANCH_WIDTH
    gate_col0 = MIX_COLS // tn
    gate_stride = D_MODEL // tn

    def gate_spec(n):
        return pl.BlockSpec((d, tn), lambda j, i: (0, gate_col0 + n * gate_stride + j))

    return pl.pallas_call(
        _gate_merge_kernel,
        grid=(D_MODEL // tn, m // tm),
        in_specs=[pl.BlockSpec((tm, d), lambda j, i: (i, 0)),
                  pl.BlockSpec((tm, N_BRANCHES * w), lambda j, i: (i, 0)),
                  gate_spec(0), gate_spec(1), gate_spec(2),
                  pl.BlockSpec((N_BRANCHES, w, tn), lambda j, i: (0, 0, j))],
        out_specs=pl.BlockSpec((tm, tn), lambda j, i: (i, j)),
        out_shape=jax.ShapeDtypeStruct((m, D_MODEL), BF16),
        compiler_params=_params("arbitrary", "arbitrary"),
    )(xn, y, w_in, w_in, w_in, w_branch)


def _swiglu_up_kernel(x_ref, wg_ref, wu_ref, o_ref):
    x = x_ref[...]
    gate = jnp.dot(x, wg_ref[...], preferred_element_type=F32)
    up = jnp.dot(x, wu_ref[...], preferred_element_type=F32)
    o_ref[...] = (jax.nn.silu(gate) * up).astype(o_ref.dtype)


def swiglu_up(x, w_gate, w_up, tm=1024, tn=512):
    m, k = x.shape
    n = w_gate.shape[1]
    w_spec = pl.BlockSpec((k, tn), lambda j, i: (0, j))
    return pl.pallas_call(
        _swiglu_up_kernel,
        grid=(n // tn, m // tm),
        in_specs=[pl.BlockSpec((tm, k), lambda j, i: (i, 0)), w_spec, w_spec],
        out_specs=pl.BlockSpec((tm, tn), lambda j, i: (i, j)),
        out_shape=jax.ShapeDtypeStruct((m, n), BF16),
        compiler_params=_params("arbitrary", "arbitrary"),
    )(x, w_gate, w_up)


def kernel(x, norm_mix, w_in, pool_w, pool_scale, sgu_norm, sgu_ws, sgu_b, conv_w, w_branch, w_out, norm_ffn, dense_w_gate, dense_w_up, dense_w_down, router_w, router_b, moe_w_gate, moe_w_up, moe_w_down, norm_final):
    bsz, seq, d = x.shape
    h = x.reshape(bsz * seq, d)
    for layer in range(DEPTH):
        w_in_l = w_in[layer].astype(BF16)
        xn = rmsnorm(h, norm_mix[layer], BF16)
        p6 = matmul_cols(xn, w_in_l, MIX_COLS, BF16)
        y = token_mixers(p6, pool_w[layer], pool_scale[layer], sgu_norm[layer], sgu_ws[layer],
                         sgu_b[layer], conv_w[layer], bsz, seq)
        merged = gate_merge(xn, y, w_in_l, w_branch[layer].astype(BF16))
        h = matmul_residual(merged, w_out[layer].astype(BF16), h)
        j = layer // 2
        if layer % 2 == 0:
            hn = rmsnorm(h, norm_ffn[layer], BF16)
            act = swiglu_up(hn, dense_w_gate[j].astype(BF16), dense_w_up[j].astype(BF16))
            h = matmul_residual(act, dense_w_down[j].astype(BF16), h)
        else:
            h = moe_ffn(h, norm_ffn[layer], router_w[j], router_b[j],
                        moe_w_gate[j].astype(BF16), moe_w_up[j].astype(BF16),
                        moe_w_down[j].astype(BF16))
    return rmsnorm(h, norm_final, F32).reshape(bsz, seq, d)
```

```python
import functools

import jax
import jax.numpy as jnp
from jax import lax
from jax.experimental import pallas as pl
from jax.experimental.pallas import tpu as pltpu

D_MODEL = 4096
DEPTH = 4
BRANCH_WIDTH = D_MODEL // 4
N_BRANCHES = 3
POOL_WINDOWS = (2, 4, 8, 16)
POOL_GROUPS = 4
POOL_GROUP_DIM = BRANCH_WIDTH // POOL_GROUPS
SGU_CHUNK = 128
SGU_HEADS = 4
SGU_HEAD_DIM = BRANCH_WIDTH // SGU_HEADS
CONV_WIDTH = 3
MIX_COLS = 6 * BRANCH_WIDTH
D_FF = (3 * D_MODEL) // 2
N_EXPERTS = 8
TOP_K = 2
D_FF_EXPERT = D_FF // 4
EPS = 1e-6

V7X_VMEM_LIMIT_BYTES = 58 * 1024 * 1024
HALO_ROWS = 16

BF16 = jnp.bfloat16
F32 = jnp.float32


def _params(*semantics):
    return pltpu.CompilerParams(dimension_semantics=semantics,
                                vmem_limit_bytes=V7X_VMEM_LIMIT_BYTES)


def _rms(x, g):
    ms = jnp.mean(x * x, axis=-1, keepdims=True)
    return x * lax.rsqrt(ms + EPS) * g


def _rmsnorm_kernel(x_ref, g_ref, o_ref):
    o_ref[...] = _rms(x_ref[...], g_ref[...]).astype(o_ref.dtype)


def rmsnorm(x, g, out_dtype, tr=256):
    n, d = x.shape
    return pl.pallas_call(
        _rmsnorm_kernel,
        grid=(n // tr,),
        in_specs=[pl.BlockSpec((tr, d), lambda i: (i, 0)),
                  pl.BlockSpec((1, d), lambda i: (0, 0))],
        out_specs=pl.BlockSpec((tr, d), lambda i: (i, 0)),
        out_shape=jax.ShapeDtypeStruct((n, d), out_dtype),
        compiler_params=_params("arbitrary"),
    )(x, g.reshape(1, d))


MOE_TILE = 512
HIGH_HALF = 0xFFFF0000


def _pack_bf16_pairs(x):
    half = x.shape[1] // 2
    hi = lax.bitcast_convert_type(x[:, :half].astype(BF16).astype(F32), jnp.uint32)
    lo = lax.bitcast_convert_type(x[:, half:].astype(BF16).astype(F32), jnp.uint32)
    return (hi & jnp.uint32(HIGH_HALF)) | (lo >> 16)


def _unpack_bf16_pairs(p):
    hi = lax.bitcast_convert_type(p & jnp.uint32(HIGH_HALF), F32)
    lo = lax.bitcast_convert_type(p << 16, F32)
    return hi, lo


def _rmsnorm_router_kernel(x_ref, g_ref, rw_ref, rb_ref, o_ref, code_ref, prob_ref, cnt_ref,
                           carry_ref, *, n_tokens):
    i = pl.program_id(0)
    tr = x_ref.shape[0]

    @pl.when(i == 0)
    def _():
        carry_ref[...] = jnp.zeros_like(carry_ref)

    hn = _rms(x_ref[...], g_ref[...])
    o_ref[...] = _pack_bf16_pairs(hn)
    logits = jnp.dot(hn, rw_ref[...], precision=lax.Precision.HIGHEST,
                     preferred_element_type=F32) + rb_ref[...]
    idx = lax.broadcasted_iota(jnp.int32, logits.shape, 1)
    m1 = jnp.max(logits, axis=1, keepdims=True)
    i1 = jnp.min(jnp.where(logits == m1, idx, N_EXPERTS), axis=1, keepdims=True)
    sel1 = idx == i1
    rest = jnp.where(sel1, -jnp.inf, logits)
    m2 = jnp.max(rest, axis=1, keepdims=True)
    i2 = jnp.min(jnp.where((rest == m2) & jnp.logical_not(sel1), idx, N_EXPERTS),
                 axis=1, keepdims=True)
    sel2 = idx == i2
    e2 = jnp.exp(m2 - m1)
    denom = 1.0 + e2
    two = lax.broadcasted_iota(jnp.int32, (tr, TOP_K), 1)
    prob_ref[...] = jnp.where(two == 0, 1.0 / denom, e2 / denom)

    chosen = (sel1 | sel2).astype(F32)
    r = lax.broadcasted_iota(jnp.int32, (tr, tr), 0)
    c = lax.broadcasted_iota(jnp.int32, (tr, tr), 1)
    earlier = (c < r).astype(BF16)
    before = jnp.dot(earlier, chosen.astype(BF16), preferred_element_type=F32) + carry_ref[...]
    rank1 = jnp.sum(jnp.where(sel1, before, 0.0), axis=1, keepdims=True).astype(jnp.int32)
    rank2 = jnp.sum(jnp.where(sel2, before, 0.0), axis=1, keepdims=True).astype(jnp.int32)
    code_ref[...] = jnp.where(two == 0, i1 * n_tokens + rank1, i2 * n_tokens + rank2)
    total = carry_ref[...] + jnp.sum(chosen, axis=0, keepdims=True)
    carry_ref[...] = total
    cnt_ref[...] = total.astype(jnp.int32)


def rmsnorm_router(x, g, router_w, router_b, tr=256):
    n, d = x.shape
    return pl.pallas_call(
        functools.partial(_rmsnorm_router_kernel, n_tokens=n),
        grid=(n // tr,),
        in_specs=[pl.BlockSpec((tr, d), lambda i: (i, 0)),
                  pl.BlockSpec((1, d), lambda i: (0, 0)),
                  pl.BlockSpec((d, N_EXPERTS), lambda i: (0, 0)),
                  pl.BlockSpec((1, N_EXPERTS), lambda i: (0, 0))],
        out_specs=[pl.BlockSpec((tr, d // 2), lambda i: (i, 0)),
                   pl.BlockSpec((tr, TOP_K), lambda i: (i, 0)),
                   pl.BlockSpec((tr, TOP_K), lambda i: (i, 0)),
                   pl.BlockSpec((1, N_EXPERTS), lambda i: (0, 0))],
        out_shape=[jax.ShapeDtypeStruct((n, d // 2), jnp.uint32),
                   jax.ShapeDtypeStruct((n, TOP_K), jnp.int32),
                   jax.ShapeDtypeStruct((n, TOP_K), F32),
                   jax.ShapeDtypeStruct((1, N_EXPERTS), jnp.int32)],
        scratch_shapes=[pltpu.VMEM((1, N_EXPERTS), F32)],
        compiler_params=_params("arbitrary"),
        name="rmsnorm_router",
    )(x, g.reshape(1, d), router_w, router_b.reshape(1, N_EXPERTS))


def _row_copy(src_ref, src_row, dst_ref, dst_row, sem, rows=1):
    return pltpu.make_async_copy(src_ref.at[pl.ds(src_row, rows)],
                                 dst_ref.at[pl.ds(dst_row, rows)], sem)


def _dispatch_kernel(code_ref, cnt_ref, src_ref, dst_ref, slot_ref, tile_e_ref, nused_ref,
                     off_ref, sem, *, n_tokens, n_tiles):
    step = pl.program_id(0)
    tr = src_ref.shape[0]
    shift = n_tokens.bit_length() - 1
    round_up = lambda v: ((v + MOE_TILE - 1) // MOE_TILE) * MOE_TILE

    @pl.when(step == 0)
    def _():
        off_ref[0] = 0
        for e in range(N_EXPERTS):
            off_ref[e + 1] = off_ref[e] + round_up(cnt_ref[e])
        n_used = off_ref[N_EXPERTS] // MOE_TILE
        nused_ref[0] = n_used

        def tile_body(i, carry):
            row = jnp.minimum(i, n_used - 1) * MOE_TILE
            e = jnp.int32(0)
            for k in range(1, N_EXPERTS):
                e = e + (row >= off_ref[k]).astype(jnp.int32)
            tile_e_ref[i] = e
            return carry
        lax.fori_loop(0, n_tiles, tile_body, 0)

    def issue(k, carry):
        for kk in range(TOP_K):
            i = TOP_K * (step * tr + k) + kk
            code = code_ref[i]
            slot = off_ref[code >> shift] + (code & (n_tokens - 1))
            slot_ref[i] = slot
            _row_copy(src_ref, k, dst_ref, slot, sem).start()
        return carry
    lax.fori_loop(0, tr, issue, 0, unroll=4)
    _row_copy(dst_ref, 0, dst_ref, 0, sem, TOP_K * tr).wait()

    @pl.when(step == pl.num_programs(0) - 1)
    def _():
        def fill(lo, hi):
            def pad_start(row, carry):
                _row_copy(src_ref, 0, dst_ref, row, sem).start()
                return carry
            lax.fori_loop(lo, hi, pad_start, 0)

            def pad_wait(row, carry):
                _row_copy(src_ref, 0, dst_ref, 0, sem).wait()
                return carry
            lax.fori_loop(lo, hi, pad_wait, 0)

        for e in range(N_EXPERTS):
            fill(off_ref[e] + cnt_ref[e], off_ref[e + 1])
        fill(off_ref[N_EXPERTS], n_tiles * MOE_TILE)


def moe_dispatch(codes, counts, rows, tr=256):
    n, width = rows.shape
    assert n & (n - 1) == 0
    n_tiles = (TOP_K * n) // MOE_TILE + N_EXPERTS
    smem = pl.BlockSpec(memory_space=pltpu.SMEM)
    return pl.pallas_call(
        functools.partial(_dispatch_kernel, n_tokens=n, n_tiles=n_tiles),
        grid_spec=pltpu.PrefetchScalarGridSpec(
            num_scalar_prefetch=2, grid=(n // tr,),
            in_specs=[pl.BlockSpec((tr, width), lambda i, code, cnt: (i, 0))],
            out_specs=[pl.BlockSpec(memory_space=pl.ANY), smem, smem, smem],
            scratch_shapes=[pltpu.SMEM((N_EXPERTS + 1,), jnp.int32),
                            pltpu.SemaphoreType.DMA(())]),
        out_shape=[jax.ShapeDtypeStruct((n_tiles * MOE_TILE, width), rows.dtype),
                   jax.ShapeDtypeStruct((TOP_K * n,), jnp.int32),
                   jax.ShapeDtypeStruct((n_tiles,), jnp.int32),
                   jax.ShapeDtypeStruct((1,), jnp.int32)],
        compiler_params=_params("arbitrary"),
        name="moe_dispatch",
    )(codes.reshape(TOP_K * n), counts.reshape(N_EXPERTS), rows)


def _moe_up_kernel(tile_e_ref, nused_ref, x_ref, wg_ref, wu_ref, o_ref):
    del tile_e_ref

    @pl.when(pl.program_id(1) < nused_ref[0])
    def _():
        half = x_ref.shape[1]
        hi, lo = _unpack_bf16_pairs(x_ref[...])
        hi = hi.astype(BF16)
        lo = lo.astype(BF16)
        gate = (jnp.dot(hi, wg_ref[:half, :], preferred_element_type=F32)
                + jnp.dot(lo, wg_ref[half:, :], preferred_element_type=F32))
        up = (jnp.dot(hi, wu_ref[:half, :], preferred_element_type=F32)
              + jnp.dot(lo, wu_ref[half:, :], preferred_element_type=F32))
        o_ref[...] = (jax.nn.silu(gate) * up).astype(o_ref.dtype)

    @pl.when(pl.program_id(1) >= nused_ref[0])
    def _():
        o_ref[...] = jnp.zeros_like(o_ref)


def moe_up(xs, tile_e, n_used, w_gate, w_up, layer, tn=512):
    p, half = xs.shape
    _, n_exp, k, f = w_gate.shape
    n_tiles = p // MOE_TILE
    row_tile = lambda j, i, te, nu: (jnp.minimum(i, nu[0] - 1), 0)
    w_spec = pl.BlockSpec((None, None, k, tn), lambda j, i, te, nu: (layer, te[i], 0, j))
    return pl.pallas_call(
        _moe_up_kernel,
        grid_spec=pltpu.PrefetchScalarGridSpec(
            num_scalar_prefetch=2, grid=(f // tn, n_tiles),
            in_specs=[pl.BlockSpec((MOE_TILE, half), row_tile), w_spec, w_spec],
            out_specs=pl.BlockSpec((MOE_TILE, tn), lambda j, i, te, nu: (i, j))),
        out_shape=jax.ShapeDtypeStruct((p, f), BF16),
        compiler_params=_params("arbitrary", "arbitrary"),
        name="moe_up",
    )(tile_e, n_used, xs, w_gate, w_up)


def _moe_down_kernel(tile_e_ref, nused_ref, a_ref, w_ref, o_ref):
    del tile_e_ref

    @pl.when(pl.program_id(0) < nused_ref[0])
    def _():
        y = jnp.dot(a_ref[...], w_ref[...], preferred_element_type=F32)
        o_ref[...] = _pack_bf16_pairs(y)

    @pl.when(pl.program_id(0) >= nused_ref[0])
    def _():
        o_ref[...] = jnp.zeros_like(o_ref)


def moe_down(act, tile_e, n_used, w_down, layer):
    p, f = act.shape
    d = w_down.shape[3]
    row_tile = lambda i, te, nu: (jnp.minimum(i, nu[0] - 1), 0)
    return pl.pallas_call(
        _moe_down_kernel,
        grid_spec=pltpu.PrefetchScalarGridSpec(
            num_scalar_prefetch=2, grid=(p // MOE_TILE,),
            in_specs=[pl.BlockSpec((MOE_TILE, f), row_tile),
                      pl.BlockSpec((None, None, f, d), lambda i, te, nu: (layer, te[i], 0, 0))],
            out_specs=pl.BlockSpec((MOE_TILE, d // 2), lambda i, te, nu: (i, 0))),
        out_shape=jax.ShapeDtypeStruct((p, d // 2), jnp.uint32),
        compiler_params=_params("arbitrary"),
        name="moe_down",
    )(tile_e, n_used, act, w_down)


def _moe_combine_kernel(slot_ref, h_ref, p_ref, ys_ref, o_ref, buf_ref, sems):
    i = pl.program_id(0)
    tr = h_ref.shape[0]
    half = ys_ref.shape[1]

    def start_gather(tile, b):
        def issue(k, carry):
            for kk in range(TOP_K):
                slot = slot_ref[TOP_K * (tile * tr + k) + kk]
                _row_copy(ys_ref, slot, buf_ref.at[b, kk], k, sems.at[b]).start()
            return carry
        lax.fori_loop(0, tr, issue, 0, unroll=4)

    @pl.when(i == 0)
    def _():
        start_gather(0, 0)

    @pl.when(i + 1 < pl.num_programs(0))
    def _():
        start_gather(i + 1, (i + 1) % 2)

    b = i % 2
    for kk in range(TOP_K):
        _row_copy(ys_ref, 0, buf_ref.at[b, kk], 0, sems.at[b], tr).wait()

    p1 = p_ref[:, 0:1]
    p2 = p_ref[:, 1:2]
    hi1, lo1 = _unpack_bf16_pairs(buf_ref[b, 0])
    hi2, lo2 = _unpack_bf16_pairs(buf_ref[b, 1])
    o_ref[:, :half] = h_ref[:, :half] + (p1 * hi1 + p2 * hi2)
    o_ref[:, half:] = h_ref[:, half:] + (p1 * lo1 + p2 * lo2)


def moe_combine(h, slots, ys, probs, tr=256):
    n, d = h.shape
    width = ys.shape[1]
    return pl.pallas_call(
        _moe_combine_kernel,
        grid_spec=pltpu.PrefetchScalarGridSpec(
            num_scalar_prefetch=1, grid=(n // tr,),
            in_specs=[pl.BlockSpec((tr, d), lambda i, s: (i, 0)),
                      pl.BlockSpec((tr, TOP_K), lambda i, s: (i, 0)),
                      pl.BlockSpec(memory_space=pl.ANY)],
            out_specs=pl.BlockSpec((tr, d), lambda i, s: (i, 0)),
            scratch_shapes=[pltpu.VMEM((2, TOP_K, tr, width), ys.dtype),
                            pltpu.SemaphoreType.DMA((2,))]),
        out_shape=jax.ShapeDtypeStruct((n, d), F32),
        input_output_aliases={1: 0},
        compiler_params=_params("arbitrary"),
        name="moe_combine",
    )(slots, h, probs, ys)


def moe_ffn(h, norm_g, router_w, router_b, w_gate, w_up, w_down, layer):
    rows, codes, probs, counts = rmsnorm_router(h, norm_g, router_w, router_b)
    xs, slots, tile_e, n_used = moe_dispatch(codes, counts, rows)
    act = moe_up(xs, tile_e, n_used, w_gate, w_up, layer)
    ys = moe_down(act, tile_e, n_used, w_down, layer)
    return moe_combine(h, slots, ys, probs)


def _matmul_cast_weight_kernel(x_ref, w_ref, o_ref, wb_ref):
    @pl.when(pl.program_id(1) == 0)
    def _():
        wb_ref[...] = w_ref[...].astype(BF16)

    o_ref[...] = jnp.dot(x_ref[...], wb_ref[...],
                         preferred_element_type=F32).astype(o_ref.dtype)


def matmul_cols(x, w, layer, n_cols, out_dtype, tm=1024, tn=768):
    m, k = x.shape
    return pl.pallas_call(
        _matmul_cast_weight_kernel,
        grid=(n_cols // tn, m // tm),
        in_specs=[pl.BlockSpec((tm, k), lambda j, i: (i, 0)),
                  pl.BlockSpec((None, k, tn), lambda j, i: (layer, 0, j))],
        out_specs=pl.BlockSpec((tm, tn), lambda j, i: (i, j)),
        out_shape=jax.ShapeDtypeStruct((m, n_cols), out_dtype),
        scratch_shapes=[pltpu.VMEM((k, tn), BF16)],
        compiler_params=_params("arbitrary", "arbitrary"),
        name="in_proj",
    )(x, w)


def _matmul_residual_kernel(x_ref, w_ref, r_ref, o_ref):
    o_ref[...] = r_ref[...] + jnp.dot(x_ref[...], w_ref[...], preferred_element_type=F32)


def matmul_residual(x, w, layer, res, tm, tn, in_place=True):
    m, k = x.shape
    n = w.shape[2]
    return pl.pallas_call(
        _matmul_residual_kernel,
        grid=(n // tn, m // tm),
        in_specs=[pl.BlockSpec((tm, k), lambda j, i: (i, 0)),
                  pl.BlockSpec((None, k, tn), lambda j, i: (layer, 0, j)),
                  pl.BlockSpec((tm, tn), lambda j, i: (i, j))],
        out_specs=pl.BlockSpec((tm, tn), lambda j, i: (i, j)),
        out_shape=jax.ShapeDtypeStruct((m, n), F32),
        input_output_aliases={2: 0} if in_place else {},
        compiler_params=_params("arbitrary", "arbitrary"),
        name="matmul_residual",
    )(x, w, res)


def _gelu(x):
    return 0.5 * x * (1.0 + lax.erf(x * (2.0 ** -0.5)))


def _shift_rows(x, k):
    return pltpu.roll(x, k, axis=0)


def _mixer_kernel(p_ref, poolw_ref, pscale_ref, sgunorm_ref, ws_ref, sb_ref, convw_ref,
                  y_ref, halo_ref, *, ts):
    w = BRANCH_WIDTH
    s_idx = pl.program_id(1)

    @pl.when(s_idx == 0)
    def _():
        halo_ref[...] = jnp.zeros_like(halo_ref)

    a = p_ref[:, 0:w].astype(F32)
    ext_a = jnp.concatenate([halo_ref[:, 0:w], a], axis=0)
    pos = s_idx * ts + lax.broadcasted_iota(jnp.int32, (ts, 1), 0)
    for g, win in enumerate(POOL_WINDOWS):
        cols = slice(g * POOL_GROUP_DIM, (g + 1) * POOL_GROUP_DIM)
        acc = ext_a[:, cols]
        step = 1
        while step < win:
            acc = acc + _shift_rows(acc, step)
            step *= 2
        cnt = jnp.minimum(pos + 1, win).astype(F32)
        pooled = acc[HALO_ROWS:, :] / cnt - a[:, cols]
        ya = jnp.dot(pooled.astype(BF16), poolw_ref[g], preferred_element_type=F32)
        y_ref[:, cols] = (ya * pscale_ref[:, cols]).astype(y_ref.dtype)

    u = _gelu(p_ref[:, w:2 * w].astype(F32))
    v = _rms(_gelu(p_ref[:, 2 * w:3 * w].astype(F32)), sgunorm_ref[...]).astype(BF16)
    row = lax.broadcasted_iota(jnp.int32, (SGU_CHUNK, SGU_CHUNK), 0)
    col = lax.broadcasted_iota(jnp.int32, (SGU_CHUNK, SGU_CHUNK), 1)
    for g in range(SGU_HEADS):
        cols = slice(g * SGU_HEAD_DIM, (g + 1) * SGU_HEAD_DIM)
        w_causal = jnp.where(row >= col, ws_ref[g], 0.0).astype(BF16)
        bias = sb_ref[:, g:g + 1]
        for c in range(ts // SGU_CHUNK):
            rows = slice(c * SGU_CHUNK, (c + 1) * SGU_CHUNK)
            vm = jnp.dot(w_causal, v[rows, cols], preferred_element_type=F32) + bias
            y_ref[rows, w + g * SGU_HEAD_DIM:w + (g + 1) * SGU_HEAD_DIM] = (
                u[rows, cols] * vm).astype(y_ref.dtype)

    hc = p_ref[:, 5 * w:6 * w].astype(F32) * p_ref[:, 3 * w:4 * w].astype(F32)
    ext_h = jnp.concatenate([halo_ref[:, w:2 * w], hc], axis=0)
    yc = (convw_ref[2:3, :] * ext_h
          + convw_ref[1:2, :] * _shift_rows(ext_h, 1)
          + convw_ref[0:1, :] * _shift_rows(ext_h, 2))
    y_ref[:, 2 * w:3 * w] = (p_ref[:, 4 * w:5 * w].astype(F32)
                             * yc[HALO_ROWS:, :]).astype(y_ref.dtype)

    halo_ref[:, 0:w] = a[ts - HALO_ROWS:, :]
    halo_ref[:, w:2 * w] = hc[ts - HALO_ROWS:, :]


def token_mixers(p6, pool_w, pool_scale, sgu_norm, sgu_ws, sgu_b, conv_w, bsz, seq, ts=256):
    n = p6.shape[0]
    w = BRANCH_WIDTH
    tiles = seq // ts
    row_block = lambda b, s: (b * tiles + s, 0)
    const2 = lambda b, s: (0, 0)
    const3 = lambda b, s: (0, 0, 0)
    return pl.pallas_call(
        functools.partial(_mixer_kernel, ts=ts),
        grid=(bsz, tiles),
        in_specs=[pl.BlockSpec((ts, MIX_COLS), row_block),
                  pl.BlockSpec((POOL_GROUPS, POOL_GROUP_DIM, POOL_GROUP_DIM), const3),
                  pl.BlockSpec((1, w), const2),
                  pl.BlockSpec((1, w), const2),
                  pl.BlockSpec((SGU_HEADS, SGU_CHUNK, SGU_CHUNK), const3),
                  pl.BlockSpec((SGU_CHUNK, SGU_HEADS), const2),
                  pl.BlockSpec((CONV_WIDTH, w), const2)],
        out_specs=pl.BlockSpec((ts, N_BRANCHES * w), row_block),
        out_shape=jax.ShapeDtypeStruct((n, N_BRANCHES * w), BF16),
        scratch_shapes=[pltpu.VMEM((HALO_ROWS, 2 * w), F32)],
        compiler_params=_params("arbitrary", "arbitrary"),
    )(p6, pool_w.astype(BF16), pool_scale.reshape(1, w), sgu_norm.reshape(1, w),
      sgu_ws, sgu_b.T, conv_w)


def _gate_merge_kernel(xn_ref, y_ref, wga_ref, wgb_ref, wgc_ref, wb_ref, o_ref):
    w = BRANCH_WIDTH
    xn = xn_ref[...]
    acc = None
    for n, wg_ref in enumerate((wga_ref, wgb_ref, wgc_ref)):
        gate = jnp.dot(xn, wg_ref[...], preferred_element_type=F32)
        branch = jnp.dot(y_ref[:, n * w:(n + 1) * w], wb_ref[n], preferred_element_type=F32)
        term = jax.nn.sigmoid(gate) * branch
        acc = term if acc is None else acc + term
    o_ref[...] = acc.astype(o_ref.dtype)


def gate_merge(xn, y, w_gates, w_branch, layer, tm=512, tn=512):
    m, d = xn.shape
    w = BRANCH_WIDTH
    gate_stride = D_MODEL // tn

    def gate_spec(n):
        return pl.BlockSpec((None, d, tn), lambda j, i: (layer, 0, n * gate_stride + j))

    return pl.pallas_call(
        _gate_merge_kernel,
        grid=(D_MODEL // tn, m // tm),
        in_specs=[pl.BlockSpec((tm, d), lambda j, i: (i, 0)),
                  pl.BlockSpec((tm, N_BRANCHES * w), lambda j, i: (i, 0)),
                  gate_spec(0), gate_spec(1), gate_spec(2),
                  pl.BlockSpec((None, N_BRANCHES, w, tn), lambda j, i: (layer, 0, 0, j))],
        out_specs=pl.BlockSpec((tm, tn), lambda j, i: (i, j)),
        out_shape=jax.ShapeDtypeStruct((m, D_MODEL), BF16),
        compiler_params=_params("arbitrary", "arbitrary"),
        name="gate_merge",
    )(xn, y, w_gates, w_gates, w_gates, w_branch)


def _swiglu_up_kernel(x_ref, wg_ref, wu_ref, o_ref):
    x = x_ref[...]
    gate = jnp.dot(x, wg_ref[...], preferred_element_type=F32)
    up = jnp.dot(x, wu_ref[...], preferred_element_type=F32)
    o_ref[...] = (jax.nn.silu(gate) * up).astype(o_ref.dtype)


def swiglu_up(x, w_gate, w_up, layer, tm=1024, tn=512):
    m, k = x.shape
    n = w_gate.shape[2]
    w_spec = pl.BlockSpec((None, k, tn), lambda j, i: (layer, 0, j))
    return pl.pallas_call(
        _swiglu_up_kernel,
        grid=(n // tn, m // tm),
        in_specs=[pl.BlockSpec((tm, k), lambda j, i: (i, 0)), w_spec, w_spec],
        out_specs=pl.BlockSpec((tm, tn), lambda j, i: (i, j)),
        out_shape=jax.ShapeDtypeStruct((m, n), BF16),
        compiler_params=_params("arbitrary", "arbitrary"),
        name="swiglu_up",
    )(x, w_gate, w_up)


def kernel(x, norm_mix, w_in, pool_w, pool_scale, sgu_norm, sgu_ws, sgu_b, conv_w, w_branch, w_out, norm_ffn, dense_w_gate, dense_w_up, dense_w_down, router_w, router_b, moe_w_gate, moe_w_up, moe_w_down, norm_final):
    bsz, seq, d = x.shape
    h = x.reshape(bsz * seq, d)
    w_gates = w_in[:, :, MIX_COLS:].astype(BF16)
    w_branch = w_branch.astype(BF16)
    w_out = w_out.astype(BF16)
    dense_w_gate = dense_w_gate.astype(BF16)
    dense_w_up = dense_w_up.astype(BF16)
    dense_w_down = dense_w_down.astype(BF16)
    moe_w_gate = moe_w_gate.astype(BF16)
    moe_w_up = moe_w_up.astype(BF16)
    moe_w_down = moe_w_down.astype(BF16)
    for layer in range(DEPTH):
        xn = rmsnorm(h, norm_mix[layer], BF16)
        p6 = matmul_cols(xn, w_in, layer, MIX_COLS, BF16)
        y = token_mixers(p6, pool_w[layer], pool_scale[layer], sgu_norm[layer], sgu_ws[layer],
                         sgu_b[layer], conv_w[layer], bsz, seq)
        merged = gate_merge(xn, y, w_gates, w_branch, layer)
        h = matmul_residual(merged, w_out, layer, h, tm=1024, tn=1024, in_place=layer > 0)
        j = layer // 2
        if layer % 2 == 0:
            hn = rmsnorm(h, norm_ffn[layer], BF16)
            act = swiglu_up(hn, dense_w_gate, dense_w_up, j)
            h = matmul_residual(act, dense_w_down, j, h, tm=512, tn=1024)
        else:
            h = moe_ffn(h, norm_ffn[layer], router_w[j], router_b[j],
                        moe_w_gate, moe_w_up, moe_w_down, j)
    return rmsnorm(h, norm_final, F32).reshape(bsz, seq, d)
```

```python
import functools

import jax
import jax.numpy as jnp
from jax import lax
from jax.experimental import pallas as pl
from jax.experimental.pallas import tpu as pltpu

D_MODEL = 4096
DEPTH = 4
BRANCH_WIDTH = D_MODEL // 4
N_BRANCHES = 3
POOL_WINDOWS = (2, 4, 8, 16)
POOL_GROUPS = 4
POOL_GROUP_DIM = BRANCH_WIDTH // POOL_GROUPS
SGU_CHUNK = 128
SGU_HEADS = 4
SGU_HEAD_DIM = BRANCH_WIDTH // SGU_HEADS
CONV_WIDTH = 3
MIX_COLS = 6 * BRANCH_WIDTH
D_FF = (3 * D_MODEL) // 2
N_EXPERTS = 8
TOP_K = 2
D_FF_EXPERT = D_FF // 4
EPS = 1e-6

V7X_VMEM_LIMIT_BYTES = 58 * 1024 * 1024
HALO_ROWS = 16

BF16 = jnp.bfloat16
F32 = jnp.float32


def _params(*semantics):
    return pltpu.CompilerParams(dimension_semantics=semantics,
                                vmem_limit_bytes=V7X_VMEM_LIMIT_BYTES)


def _rms(x, g):
    ms = jnp.mean(x * x, axis=-1, keepdims=True)
    return x * lax.rsqrt(ms + EPS) * g


def _rmsnorm_kernel(x_ref, g_ref, o_ref):
    o_ref[...] = _rms(x_ref[...], g_ref[...]).astype(o_ref.dtype)


def rmsnorm(x, g, out_dtype, tr=256):
    n, d = x.shape
    return pl.pallas_call(
        _rmsnorm_kernel,
        grid=(n // tr,),
        in_specs=[pl.BlockSpec((tr, d), lambda i: (i, 0)),
                  pl.BlockSpec((1, d), lambda i: (0, 0))],
        out_specs=pl.BlockSpec((tr, d), lambda i: (i, 0)),
        out_shape=jax.ShapeDtypeStruct((n, d), out_dtype),
        compiler_params=_params("arbitrary"),
    )(x, g.reshape(1, d))


MOE_TILE = 512
HIGH_HALF = 0xFFFF0000


def _pack_bf16_pairs(x):
    half = x.shape[1] // 2
    hi = lax.bitcast_convert_type(x[:, :half].astype(BF16).astype(F32), jnp.uint32)
    lo = lax.bitcast_convert_type(x[:, half:].astype(BF16).astype(F32), jnp.uint32)
    return (hi & jnp.uint32(HIGH_HALF)) | (lo >> 16)


def _unpack_bf16_pairs(p):
    hi = lax.bitcast_convert_type(p & jnp.uint32(HIGH_HALF), F32)
    lo = lax.bitcast_convert_type(p << 16, F32)
    return hi, lo


def _rmsnorm_router_kernel(x_ref, g_ref, rw_ref, rb_ref, o_ref, code_ref, prob_ref, cnt_ref,
                           carry_ref, *, n_tokens):
    i = pl.program_id(0)
    tr = x_ref.shape[0]

    @pl.when(i == 0)
    def _():
        carry_ref[...] = jnp.zeros_like(carry_ref)

    hn = _rms(x_ref[...], g_ref[...])
    o_ref[...] = _pack_bf16_pairs(hn)
    logits = jnp.dot(hn, rw_ref[...], precision=lax.Precision.HIGHEST,
                     preferred_element_type=F32) + rb_ref[...]
    idx = lax.broadcasted_iota(jnp.int32, logits.shape, 1)
    m1 = jnp.max(logits, axis=1, keepdims=True)
    i1 = jnp.min(jnp.where(logits == m1, idx, N_EXPERTS), axis=1, keepdims=True)
    sel1 = idx == i1
    rest = jnp.where(sel1, -jnp.inf, logits)
    m2 = jnp.max(rest, axis=1, keepdims=True)
    i2 = jnp.min(jnp.where((rest == m2) & jnp.logical_not(sel1), idx, N_EXPERTS),
                 axis=1, keepdims=True)
    sel2 = idx == i2
    e2 = jnp.exp(m2 - m1)
    denom = 1.0 + e2
    two = lax.broadcasted_iota(jnp.int32, (tr, TOP_K), 1)
    prob_ref[...] = jnp.where(two == 0, 1.0 / denom, e2 / denom)

    chosen = (sel1 | sel2).astype(F32)
    r = lax.broadcasted_iota(jnp.int32, (tr, tr), 0)
    c = lax.broadcasted_iota(jnp.int32, (tr, tr), 1)
    earlier = (c < r).astype(BF16)
    before = jnp.dot(earlier, chosen.astype(BF16), preferred_element_type=F32) + carry_ref[...]
    rank1 = jnp.sum(jnp.where(sel1, before, 0.0), axis=1, keepdims=True).astype(jnp.int32)
    rank2 = jnp.sum(jnp.where(sel2, before, 0.0), axis=1, keepdims=True).astype(jnp.int32)
    code_ref[...] = jnp.where(two == 0, i1 * n_tokens + rank1, i2 * n_tokens + rank2)
    total = carry_ref[...] + jnp.sum(chosen, axis=0, keepdims=True)
    carry_ref[...] = total
    cnt_ref[...] = total.astype(jnp.int32)


def rmsnorm_router(x, g, router_w, router_b, tr=256):
    n, d = x.shape
    return pl.pallas_call(
        functools.partial(_rmsnorm_router_kernel, n_tokens=n),
        grid=(n // tr,),
        in_specs=[pl.BlockSpec((tr, d), lambda i: (i, 0)),
                  pl.BlockSpec((1, d), lambda i: (0, 0)),
                  pl.BlockSpec((d, N_EXPERTS), lambda i: (0, 0)),
                  pl.BlockSpec((1, N_EXPERTS), lambda i: (0, 0))],
        out_specs=[pl.BlockSpec((tr, d // 2), lambda i: (i, 0)),
                   pl.BlockSpec((tr, TOP_K), lambda i: (i, 0)),
                   pl.BlockSpec((tr, TOP_K), lambda i: (i, 0)),
                   pl.BlockSpec((1, N_EXPERTS), lambda i: (0, 0))],
        out_shape=[jax.ShapeDtypeStruct((n, d // 2), jnp.uint32),
                   jax.ShapeDtypeStruct((n, TOP_K), jnp.int32),
                   jax.ShapeDtypeStruct((n, TOP_K), F32),
                   jax.ShapeDtypeStruct((1, N_EXPERTS), jnp.int32)],
        scratch_shapes=[pltpu.VMEM((1, N_EXPERTS), F32)],
        compiler_params=_params("arbitrary"),
        name="rmsnorm_router",
    )(x, g.reshape(1, d), router_w, router_b.reshape(1, N_EXPERTS))


def _row_copy(src_ref, src_row, dst_ref, dst_row, sem, rows=1):
    return pltpu.make_async_copy(src_ref.at[pl.ds(src_row, rows)],
                                 dst_ref.at[pl.ds(dst_row, rows)], sem)


def _dispatch_kernel(code_ref, cnt_ref, src_ref, dst_ref, slot_ref, tile_e_ref, nused_ref,
                     off_ref, sem, *, n_tokens, n_tiles):
    step = pl.program_id(0)
    tr = src_ref.shape[0]
    shift = n_tokens.bit_length() - 1
    round_up = lambda v: ((v + MOE_TILE - 1) // MOE_TILE) * MOE_TILE

    @pl.when(step == 0)
    def _():
        off_ref[0] = 0
        for e in range(N_EXPERTS):
            off_ref[e + 1] = off_ref[e] + round_up(cnt_ref[e])
        n_used = off_ref[N_EXPERTS] // MOE_TILE
        nused_ref[0] = n_used

        def tile_body(i, carry):
            row = jnp.minimum(i, n_used - 1) * MOE_TILE
            e = jnp.int32(0)
            for k in range(1, N_EXPERTS):
                e = e + (row >= off_ref[k]).astype(jnp.int32)
            tile_e_ref[i] = e
            return carry
        lax.fori_loop(0, n_tiles, tile_body, 0)

    def issue(k, carry):
        for kk in range(TOP_K):
            i = TOP_K * (step * tr + k) + kk
            code = code_ref[i]
            slot = off_ref[code >> shift] + (code & (n_tokens - 1))
            slot_ref[i] = slot
            _row_copy(src_ref, k, dst_ref, slot, sem).start()
        return carry
    lax.fori_loop(0, tr, issue, 0, unroll=4)
    _row_copy(dst_ref, 0, dst_ref, 0, sem, TOP_K * tr).wait()

    @pl.when(step == pl.num_programs(0) - 1)
    def _():
        def fill(lo, hi):
            def pad_start(row, carry):
                _row_copy(src_ref, 0, dst_ref, row, sem).start()
                return carry
            lax.fori_loop(lo, hi, pad_start, 0)

            def pad_wait(row, carry):
                _row_copy(src_ref, 0, dst_ref, 0, sem).wait()
                return carry
            lax.fori_loop(lo, hi, pad_wait, 0)

        for e in range(N_EXPERTS):
            fill(off_ref[e] + cnt_ref[e], off_ref[e + 1])
        fill(off_ref[N_EXPERTS], n_tiles * MOE_TILE)


def moe_dispatch(codes, counts, rows, tr=256):
    n, width = rows.shape
    assert n & (n - 1) == 0
    n_tiles = (TOP_K * n) // MOE_TILE + N_EXPERTS
    smem = pl.BlockSpec(memory_space=pltpu.SMEM)
    return pl.pallas_call(
        functools.partial(_dispatch_kernel, n_tokens=n, n_tiles=n_tiles),
        grid_spec=pltpu.PrefetchScalarGridSpec(
            num_scalar_prefetch=2, grid=(n // tr,),
            in_specs=[pl.BlockSpec((tr, width), lambda i, code, cnt: (i, 0))],
            out_specs=[pl.BlockSpec(memory_space=pl.ANY), smem, smem, smem],
            scratch_shapes=[pltpu.SMEM((N_EXPERTS + 1,), jnp.int32),
                            pltpu.SemaphoreType.DMA(())]),
        out_shape=[jax.ShapeDtypeStruct((n_tiles * MOE_TILE, width), rows.dtype),
                   jax.ShapeDtypeStruct((TOP_K * n,), jnp.int32),
                   jax.ShapeDtypeStruct((n_tiles,), jnp.int32),
                   jax.ShapeDtypeStruct((1,), jnp.int32)],
        compiler_params=_params("arbitrary"),
        name="moe_dispatch",
    )(codes.reshape(TOP_K * n), counts.reshape(N_EXPERTS), rows)


def _moe_up_kernel(tile_e_ref, nused_ref, x_ref, wg_ref, wu_ref, o_ref):
    del tile_e_ref

    @pl.when(pl.program_id(1) < nused_ref[0])
    def _():
        half = x_ref.shape[1]
        hi, lo = _unpack_bf16_pairs(x_ref[...])
        hi = hi.astype(BF16)
        lo = lo.astype(BF16)
        gate = (jnp.dot(hi, wg_ref[:half, :], preferred_element_type=F32)
                + jnp.dot(lo, wg_ref[half:, :], preferred_element_type=F32))
        up = (jnp.dot(hi, wu_ref[:half, :], preferred_element_type=F32)
              + jnp.dot(lo, wu_ref[half:, :], preferred_element_type=F32))
        o_ref[...] = (jax.nn.silu(gate) * up).astype(o_ref.dtype)

    @pl.when(pl.program_id(1) >= nused_ref[0])
    def _():
        o_ref[...] = jnp.zeros_like(o_ref)


def moe_up(xs, tile_e, n_used, w_gate, w_up, layer, tn=512):
    p, half = xs.shape
    _, n_exp, k, f = w_gate.shape
    n_tiles = p // MOE_TILE
    row_tile = lambda j, i, te, nu: (jnp.minimum(i, nu[0] - 1), 0)
    w_spec = pl.BlockSpec((None, None, k, tn), lambda j, i, te, nu: (layer, te[i], 0, j))
    return pl.pallas_call(
        _moe_up_kernel,
        grid_spec=pltpu.PrefetchScalarGridSpec(
            num_scalar_prefetch=2, grid=(f // tn, n_tiles),
            in_specs=[pl.BlockSpec((MOE_TILE, half), row_tile), w_spec, w_spec],
            out_specs=pl.BlockSpec((MOE_TILE, tn), lambda j, i, te, nu: (i, j))),
        out_shape=jax.ShapeDtypeStruct((p, f), BF16),
        compiler_params=_params("arbitrary", "arbitrary"),
        name="moe_up",
    )(tile_e, n_used, xs, w_gate, w_up)


def _moe_down_kernel(tile_e_ref, nused_ref, a_ref, w_ref, o_ref):
    del tile_e_ref

    @pl.when(pl.program_id(0) < nused_ref[0])
    def _():
        y = jnp.dot(a_ref[...], w_ref[...], preferred_element_type=F32)
        o_ref[...] = _pack_bf16_pairs(y)

    @pl.when(pl.program_id(0) >= nused_ref[0])
    def _():
        o_ref[...] = jnp.zeros_like(o_ref)


def moe_down(act, tile_e, n_used, w_down, layer):
    p, f = act.shape
    d = w_down.shape[3]
    row_tile = lambda i, te, nu: (jnp.minimum(i, nu[0] - 1), 0)
    return pl.pallas_call(
        _moe_down_kernel,
        grid_spec=pltpu.PrefetchScalarGridSpec(
            num_scalar_prefetch=2, grid=(p // MOE_TILE,),
            in_specs=[pl.BlockSpec((MOE_TILE, f), row_tile),
                      pl.BlockSpec((None, None, f, d), lambda i, te, nu: (layer, te[i], 0, 0))],
            out_specs=pl.BlockSpec((MOE_TILE, d // 2), lambda i, te, nu: (i, 0))),
        out_shape=jax.ShapeDtypeStruct((p, d // 2), jnp.uint32),
        compiler_params=_params("arbitrary"),
        name="moe_down",
    )(tile_e, n_used, act, w_down)


def _moe_combine_kernel(slot_ref, h_ref, p_ref, g_ref, ys_ref, h_out_ref, n_out_ref, buf_ref,
                        sems):
    i = pl.program_id(0)
    tr = h_ref.shape[0]
    half = ys_ref.shape[1]

    def start_gather(tile, b):
        def issue(k, carry):
            for kk in range(TOP_K):
                slot = slot_ref[TOP_K * (tile * tr + k) + kk]
                _row_copy(ys_ref, slot, buf_ref.at[b, kk], k, sems.at[b]).start()
            return carry
        lax.fori_loop(0, tr, issue, 0, unroll=4)

    @pl.when(i == 0)
    def _():
        start_gather(0, 0)

    @pl.when(i + 1 < pl.num_programs(0))
    def _():
        start_gather(i + 1, (i + 1) % 2)

    b = i % 2
    for kk in range(TOP_K):
        _row_copy(ys_ref, 0, buf_ref.at[b, kk], 0, sems.at[b], tr).wait()

    p1 = p_ref[:, 0:1]
    p2 = p_ref[:, 1:2]
    hi1, lo1 = _unpack_bf16_pairs(buf_ref[b, 0])
    hi2, lo2 = _unpack_bf16_pairs(buf_ref[b, 1])
    left = h_ref[:, :half] + (p1 * hi1 + p2 * hi2)
    right = h_ref[:, half:] + (p1 * lo1 + p2 * lo2)
    if h_out_ref is not None:
        h_out_ref[:, :half] = left
        h_out_ref[:, half:] = right
    if n_out_ref is not None:
        ms = (jnp.sum(left * left, axis=-1, keepdims=True)
              + jnp.sum(right * right, axis=-1, keepdims=True)) / (2 * half)
        scale = lax.rsqrt(ms + EPS)
        n_out_ref[:, :half] = (left * scale * g_ref[:, :half]).astype(n_out_ref.dtype)
        n_out_ref[:, half:] = (right * scale * g_ref[:, half:]).astype(n_out_ref.dtype)


def moe_combine(h, slots, ys, probs, next_g, final, tr=256):
    n, d = h.shape
    width = ys.shape[1]
    row_block = pl.BlockSpec((tr, d), lambda i, s: (i, 0))
    norm_shape = jax.ShapeDtypeStruct((n, d), F32 if final else BF16)
    out_specs = [row_block] if final else [row_block, row_block]
    out_shape = [norm_shape] if final else [jax.ShapeDtypeStruct((n, d), F32), norm_shape]

    def body(slot_ref, h_ref, p_ref, g_ref, ys_ref, *rest):
        outs, (buf_ref, sems) = rest[:-2], rest[-2:]
        h_out_ref, n_out_ref = (None, outs[0]) if final else outs
        _moe_combine_kernel(slot_ref, h_ref, p_ref, g_ref, ys_ref, h_out_ref, n_out_ref,
                            buf_ref, sems)

    outs = pl.pallas_call(
        body,
        grid_spec=pltpu.PrefetchScalarGridSpec(
            num_scalar_prefetch=1, grid=(n // tr,),
            in_specs=[row_block,
                      pl.BlockSpec((tr, TOP_K), lambda i, s: (i, 0)),
                      pl.BlockSpec((1, d), lambda i, s: (0, 0)),
                      pl.BlockSpec(memory_space=pl.ANY)],
            out_specs=out_specs,
            scratch_shapes=[pltpu.VMEM((2, TOP_K, tr, width), ys.dtype),
                            pltpu.SemaphoreType.DMA((2,))]),
        out_shape=out_shape,
        input_output_aliases={1: 0},
        compiler_params=_params("arbitrary"),
        name="moe_combine",
    )(slots, h, probs, next_g.reshape(1, d), ys)
    return (None, outs[0]) if final else outs


def moe_ffn(h, norm_g, router_w, router_b, w_gate, w_up, w_down, layer, next_g, final):
    rows, codes, probs, counts = rmsnorm_router(h, norm_g, router_w, router_b)
    xs, slots, tile_e, n_used = moe_dispatch(codes, counts, rows)
    act = moe_up(xs, tile_e, n_used, w_gate, w_up, layer)
    ys = moe_down(act, tile_e, n_used, w_down, layer)
    return moe_combine(h, slots, ys, probs, next_g, final)


def _matmul_cast_weight_kernel(x_ref, w_ref, o_ref, wb_ref):
    @pl.when(pl.program_id(1) == 0)
    def _():
        wb_ref[...] = w_ref[...].astype(BF16)

    o_ref[...] = jnp.dot(x_ref[...], wb_ref[...],
                         preferred_element_type=F32).astype(o_ref.dtype)


def matmul_cols(x, w, layer, n_cols, out_dtype, tm=1024, tn=768):
    m, k = x.shape
    return pl.pallas_call(
        _matmul_cast_weight_kernel,
        grid=(n_cols // tn, m // tm),
        in_specs=[pl.BlockSpec((tm, k), lambda j, i: (i, 0)),
                  pl.BlockSpec((None, k, tn), lambda j, i: (layer, 0, j))],
        out_specs=pl.BlockSpec((tm, tn), lambda j, i: (i, j)),
        out_shape=jax.ShapeDtypeStruct((m, n_cols), out_dtype),
        scratch_shapes=[pltpu.VMEM((k, tn), BF16)],
        compiler_params=_params("arbitrary", "arbitrary"),
        name="in_proj",
    )(x, w)


def _matmul_residual_kernel(x_ref, w_ref, r_ref, o_ref):
    o_ref[...] = r_ref[...] + jnp.dot(x_ref[...], w_ref[...], preferred_element_type=F32)


def matmul_residual(x, w, layer, res, tm, tn, in_place=True):
    m, k = x.shape
    n = w.shape[2]
    return pl.pallas_call(
        _matmul_residual_kernel,
        grid=(n // tn, m // tm),
        in_specs=[pl.BlockSpec((tm, k), lambda j, i: (i, 0)),
                  pl.BlockSpec((None, k, tn), lambda j, i: (layer, 0, j)),
                  pl.BlockSpec((tm, tn), lambda j, i: (i, j))],
        out_specs=pl.BlockSpec((tm, tn), lambda j, i: (i, j)),
        out_shape=jax.ShapeDtypeStruct((m, n), F32),
        input_output_aliases={2: 0} if in_place else {},
        compiler_params=_params("arbitrary", "arbitrary"),
        name="matmul_residual",
    )(x, w, res)


def _gelu(x):
    return 0.5 * x * (1.0 + lax.erf(x * (2.0 ** -0.5)))


def _shift_rows(x, k):
    return pltpu.roll(x, k, axis=0)


def _mixer_kernel(p_ref, poolw_ref, pscale_ref, sgunorm_ref, ws_ref, sb_ref, convw_ref,
                  y_ref, halo_ref, *, ts):
    w = BRANCH_WIDTH
    s_idx = pl.program_id(1)

    @pl.when(s_idx == 0)
    def _():
        halo_ref[...] = jnp.zeros_like(halo_ref)

    a = p_ref[:, 0:w].astype(F32)
    ext_a = jnp.concatenate([halo_ref[:, 0:w], a], axis=0)
    pos = s_idx * ts + lax.broadcasted_iota(jnp.int32, (ts, 1), 0)
    for g, win in enumerate(POOL_WINDOWS):
        cols = slice(g * POOL_GROUP_DIM, (g + 1) * POOL_GROUP_DIM)
        acc = ext_a[:, cols]
        step = 1
        while step < win:
            acc = acc + _shift_rows(acc, step)
            step *= 2
        cnt = jnp.minimum(pos + 1, win).astype(F32)
        pooled = acc[HALO_ROWS:, :] / cnt - a[:, cols]
        ya = jnp.dot(pooled.astype(BF16), poolw_ref[g], preferred_element_type=F32)
        y_ref[:, cols] = (ya * pscale_ref[:, cols]).astype(y_ref.dtype)

    u = _gelu(p_ref[:, w:2 * w].astype(F32))
    v = _rms(_gelu(p_ref[:, 2 * w:3 * w].astype(F32)), sgunorm_ref[...]).astype(BF16)
    row = lax.broadcasted_iota(jnp.int32, (SGU_CHUNK, SGU_CHUNK), 0)
    col = lax.broadcasted_iota(jnp.int32, (SGU_CHUNK, SGU_CHUNK), 1)
    for g in range(SGU_HEADS):
        cols = slice(g * SGU_HEAD_DIM, (g + 1) * SGU_HEAD_DIM)
        w_causal = jnp.where(row >= col, ws_ref[g], 0.0).astype(BF16)
        bias = sb_ref[:, g:g + 1]
        for c in range(ts // SGU_CHUNK):
            rows = slice(c * SGU_CHUNK, (c + 1) * SGU_CHUNK)
            vm = jnp.dot(w_causal, v[rows, cols], preferred_element_type=F32) + bias
            y_ref[rows, w + g * SGU_HEAD_DIM:w + (g + 1) * SGU_HEAD_DIM] = (
                u[rows, cols] * vm).astype(y_ref.dtype)

    hc = p_ref[:, 5 * w:6 * w].astype(F32) * p_ref[:, 3 * w:4 * w].astype(F32)
    ext_h = jnp.concatenate([halo_ref[:, w:2 * w], hc], axis=0)
    yc = (convw_ref[2:3, :] * ext_h
          + convw_ref[1:2, :] * _shift_rows(ext_h, 1)
          + convw_ref[0:1, :] * _shift_rows(ext_h, 2))
    y_ref[:, 2 * w:3 * w] = (p_ref[:, 4 * w:5 * w].astype(F32)
                             * yc[HALO_ROWS:, :]).astype(y_ref.dtype)

    halo_ref[:, 0:w] = a[ts - HALO_ROWS:, :]
    halo_ref[:, w:2 * w] = hc[ts - HALO_ROWS:, :]


def token_mixers(p6, pool_w, pool_scale, sgu_norm, sgu_ws, sgu_b, conv_w, bsz, seq, ts=256):
    n = p6.shape[0]
    w = BRANCH_WIDTH
    tiles = seq // ts
    row_block = lambda b, s: (b * tiles + s, 0)
    const2 = lambda b, s: (0, 0)
    const3 = lambda b, s: (0, 0, 0)
    return pl.pallas_call(
        functools.partial(_mixer_kernel, ts=ts),
        grid=(bsz, tiles),
        in_specs=[pl.BlockSpec((ts, MIX_COLS), row_block),
                  pl.BlockSpec((POOL_GROUPS, POOL_GROUP_DIM, POOL_GROUP_DIM), const3),
                  pl.BlockSpec((1, w), const2),
                  pl.BlockSpec((1, w), const2),
                  pl.BlockSpec((SGU_HEADS, SGU_CHUNK, SGU_CHUNK), const3),
                  pl.BlockSpec((SGU_CHUNK, SGU_HEADS), const2),
                  pl.BlockSpec((CONV_WIDTH, w), const2)],
        out_specs=pl.BlockSpec((ts, N_BRANCHES * w), row_block),
        out_shape=jax.ShapeDtypeStruct((n, N_BRANCHES * w), BF16),
        scratch_shapes=[pltpu.VMEM((HALO_ROWS, 2 * w), F32)],
        compiler_params=_params("arbitrary", "arbitrary"),
    )(p6, pool_w.astype(BF16), pool_scale.reshape(1, w), sgu_norm.reshape(1, w),
      sgu_ws, sgu_b.T, conv_w)


LANES = 128


def _gate_merge_kernel(xn_ref, y_ref, wga_ref, wgb_ref, wgc_ref, wb_ref, *rest, round_blocks):
    w = BRANCH_WIDTH
    o_ref = rest[-2] if round_blocks else rest[-1]
    xn = xn_ref[...]
    acc = None
    for n, wg_ref in enumerate((wga_ref, wgb_ref, wgc_ref)):
        gate = jnp.dot(xn, wg_ref[...], preferred_element_type=F32)
        branch = jnp.dot(y_ref[:, n * w:(n + 1) * w], wb_ref[n], preferred_element_type=F32)
        term = jax.nn.sigmoid(gate) * branch
        acc = term if acc is None else acc + term
    o_ref[...] = acc.astype(o_ref.dtype)

    if round_blocks:
        next_w_ref, next_o_ref = rest[0], rest[-1]

        @pl.when(pl.program_id(1) < round_blocks)
        def _():
            next_o_ref[...] = next_w_ref[...].astype(BF16)


def gate_merge(xn, y, w_gates, w_branch, layer, w_in=None, tm=512, tn=512):
    m, d = xn.shape
    w = BRANCH_WIDTH
    gate_stride = D_MODEL // tn
    n_col, n_row = D_MODEL // tn, m // tm
    gate_cols = N_BRANCHES * D_MODEL

    def gate_spec(n):
        return pl.BlockSpec((d, tn), lambda j, i: (0, n * gate_stride + j))

    in_specs = [pl.BlockSpec((tm, d), lambda j, i: (i, 0)),
                pl.BlockSpec((tm, N_BRANCHES * w), lambda j, i: (i, 0)),
                gate_spec(0), gate_spec(1), gate_spec(2),
                pl.BlockSpec((None, N_BRANCHES, w, tn), lambda j, i: (layer, 0, 0, j))]
    out_specs = [pl.BlockSpec((tm, tn), lambda j, i: (i, j))]
    out_shape = [jax.ShapeDtypeStruct((m, D_MODEL), BF16)]
    operands = [xn, y, w_gates, w_gates, w_gates, w_branch]
    round_blocks = 0
    if w_in is not None:
        round_blocks = gate_cols // LANES // n_col
        assert round_blocks * n_col * LANES == gate_cols and round_blocks <= n_row
        block = lambda j, i: j * round_blocks + jnp.minimum(i, round_blocks - 1)
        first = MIX_COLS // LANES
        in_specs.append(pl.BlockSpec((None, d, LANES),
                                     lambda j, i: (layer + 1, 0, first + block(j, i))))
        out_specs.append(pl.BlockSpec((d, LANES), lambda j, i: (0, block(j, i))))
        out_shape.append(jax.ShapeDtypeStruct((d, gate_cols), BF16))
        operands.append(w_in)
    outs = pl.pallas_call(
        functools.partial(_gate_merge_kernel, round_blocks=round_blocks),
        grid=(n_col, n_row),
        in_specs=in_specs,
        out_specs=out_specs,
        out_shape=out_shape,
        compiler_params=_params("arbitrary", "arbitrary"),
        name="gate_merge",
    )(*operands)
    return outs if round_blocks else (outs[0], None)


def _swiglu_up_kernel(x_ref, wg_ref, wu_ref, o_ref):
    x = x_ref[...]
    gate = jnp.dot(x, wg_ref[...], preferred_element_type=F32)
    up = jnp.dot(x, wu_ref[...], preferred_element_type=F32)
    o_ref[...] = (jax.nn.silu(gate) * up).astype(o_ref.dtype)


def swiglu_up(x, w_gate, w_up, layer, tm=1024, tn=512):
    m, k = x.shape
    n = w_gate.shape[2]
    w_spec = pl.BlockSpec((None, k, tn), lambda j, i: (layer, 0, j))
    return pl.pallas_call(
        _swiglu_up_kernel,
        grid=(n // tn, m // tm),
        in_specs=[pl.BlockSpec((tm, k), lambda j, i: (i, 0)), w_spec, w_spec],
        out_specs=pl.BlockSpec((tm, tn), lambda j, i: (i, j)),
        out_shape=jax.ShapeDtypeStruct((m, n), BF16),
        compiler_params=_params("arbitrary", "arbitrary"),
        name="swiglu_up",
    )(x, w_gate, w_up)


def kernel(x, norm_mix, w_in, pool_w, pool_scale, sgu_norm, sgu_ws, sgu_b, conv_w, w_branch, w_out, norm_ffn, dense_w_gate, dense_w_up, dense_w_down, router_w, router_b, moe_w_gate, moe_w_up, moe_w_down, norm_final):
    bsz, seq, d = x.shape
    h = x.reshape(bsz * seq, d)
    w_gates = w_in[0, :, MIX_COLS:].astype(BF16)
    w_branch = w_branch.astype(BF16)
    w_out = w_out.astype(BF16)
    dense_w_gate = dense_w_gate.astype(BF16)
    dense_w_up = dense_w_up.astype(BF16)
    dense_w_down = dense_w_down.astype(BF16)
    moe_w_gate = moe_w_gate.astype(BF16)
    moe_w_up = moe_w_up.astype(BF16)
    moe_w_down = moe_w_down.astype(BF16)
    xn = None
    for layer in range(DEPTH):
        last = layer == DEPTH - 1
        if xn is None:
            xn = rmsnorm(h, norm_mix[layer], BF16)
        p6 = matmul_cols(xn, w_in, layer, MIX_COLS, BF16)
        y = token_mixers(p6, pool_w[layer], pool_scale[layer], sgu_norm[layer], sgu_ws[layer],
                         sgu_b[layer], conv_w[layer], bsz, seq)
        merged, w_gates = gate_merge(xn, y, w_gates, w_branch, layer,
                                     w_in=None if last else w_in)
        h = matmul_residual(merged, w_out, layer, h, tm=1024, tn=1024, in_place=layer > 0)
        xn = None
        j = layer // 2
        if layer % 2 == 0:
            hn = rmsnorm(h, norm_ffn[layer], BF16)
            act = swiglu_up(hn, dense_w_gate, dense_w_up, j)
            h = matmul_residual(act, dense_w_down, j, h, tm=512, tn=1024)
        else:
            next_g = norm_final if last else norm_mix[layer + 1]
            h, xn = moe_ffn(h, norm_ffn[layer], router_w[j], router_b[j],
                            moe_w_gate, moe_w_up, moe_w_down, j, next_g, final=last)
    out = xn if DEPTH % 2 == 0 else rmsnorm(h, norm_final, F32)
    return out.reshape(bsz, seq, d)
```

```python
import functools

import jax
import jax.numpy as jnp
from jax import lax
from jax.experimental import pallas as pl
from jax.experimental.pallas import tpu as pltpu

D_MODEL = 4096
DEPTH = 4
BRANCH_WIDTH = D_MODEL // 4
N_BRANCHES = 3
POOL_WINDOWS = (2, 4, 8, 16)
POOL_GROUPS = 4
POOL_GROUP_DIM = BRANCH_WIDTH // POOL_GROUPS
SGU_CHUNK = 128
SGU_HEADS = 4
SGU_HEAD_DIM = BRANCH_WIDTH // SGU_HEADS
CONV_WIDTH = 3
MIX_COLS = 6 * BRANCH_WIDTH
D_FF = (3 * D_MODEL) // 2
N_EXPERTS = 8
TOP_K = 2
D_FF_EXPERT = D_FF // 4
EPS = 1e-6

V7X_VMEM_LIMIT_BYTES = 58 * 1024 * 1024
HALO_ROWS = 16

BF16 = jnp.bfloat16
F32 = jnp.float32


def _params(*semantics):
    return pltpu.CompilerParams(dimension_semantics=semantics,
                                vmem_limit_bytes=V7X_VMEM_LIMIT_BYTES)


def _rms(x, g):
    ms = jnp.mean(x * x, axis=-1, keepdims=True)
    return x * lax.rsqrt(ms + EPS) * g


def _rmsnorm_kernel(x_ref, g_ref, o_ref):
    o_ref[...] = _rms(x_ref[...], g_ref[...]).astype(o_ref.dtype)


def rmsnorm(x, g, out_dtype, tr=256):
    n, d = x.shape
    return pl.pallas_call(
        _rmsnorm_kernel,
        grid=(n // tr,),
        in_specs=[pl.BlockSpec((tr, d), lambda i: (i, 0)),
                  pl.BlockSpec((1, d), lambda i: (0, 0))],
        out_specs=pl.BlockSpec((tr, d), lambda i: (i, 0)),
        out_shape=jax.ShapeDtypeStruct((n, d), out_dtype),
        compiler_params=_params("arbitrary"),
    )(x, g.reshape(1, d))


MOE_TILE = 512
HIGH_HALF = 0xFFFF0000


def _pack_bf16_pairs(x):
    half = x.shape[1] // 2
    hi = lax.bitcast_convert_type(x[:, :half].astype(BF16).astype(F32), jnp.uint32)
    lo = lax.bitcast_convert_type(x[:, half:].astype(BF16).astype(F32), jnp.uint32)
    return (hi & jnp.uint32(HIGH_HALF)) | (lo >> 16)


def _unpack_bf16_pairs(p):
    hi = lax.bitcast_convert_type(p & jnp.uint32(HIGH_HALF), F32)
    lo = lax.bitcast_convert_type(p << 16, F32)
    return hi, lo


def _rmsnorm_router_kernel(x_ref, g_ref, rw_ref, rb_ref, o_ref, code_ref, prob_ref, cnt_ref,
                           carry_ref, *, n_tokens):
    i = pl.program_id(0)
    tr = x_ref.shape[0]

    @pl.when(i == 0)
    def _():
        carry_ref[...] = jnp.zeros_like(carry_ref)

    hn = _rms(x_ref[...], g_ref[...])
    o_ref[...] = _pack_bf16_pairs(hn)
    rw = rw_ref[...]
    hn_hi = hn.astype(BF16)
    hn_lo = (hn - hn_hi.astype(F32)).astype(BF16)
    rw_hi = rw.astype(BF16)
    rw_lo = (rw - rw_hi.astype(F32)).astype(BF16)
    logits = (jnp.dot(hn_hi, rw_hi, preferred_element_type=F32)
              + (jnp.dot(hn_hi, rw_lo, preferred_element_type=F32)
                 + jnp.dot(hn_lo, rw_hi, preferred_element_type=F32))) + rb_ref[...]
    idx = lax.broadcasted_iota(jnp.int32, logits.shape, 1)
    m1 = jnp.max(logits, axis=1, keepdims=True)
    i1 = jnp.min(jnp.where(logits == m1, idx, N_EXPERTS), axis=1, keepdims=True)
    sel1 = idx == i1
    rest = jnp.where(sel1, -jnp.inf, logits)
    m2 = jnp.max(rest, axis=1, keepdims=True)
    i2 = jnp.min(jnp.where((rest == m2) & jnp.logical_not(sel1), idx, N_EXPERTS),
                 axis=1, keepdims=True)
    sel2 = idx == i2
    e2 = jnp.exp(m2 - m1)
    denom = 1.0 + e2
    two = lax.broadcasted_iota(jnp.int32, (tr, TOP_K), 1)
    prob_ref[...] = jnp.where(two == 0, 1.0 / denom, e2 / denom)

    chosen = (sel1 | sel2).astype(F32)
    r = lax.broadcasted_iota(jnp.int32, (tr, tr), 0)
    c = lax.broadcasted_iota(jnp.int32, (tr, tr), 1)
    earlier = (c < r).astype(BF16)
    before = jnp.dot(earlier, chosen.astype(BF16), preferred_element_type=F32) + carry_ref[...]
    rank1 = jnp.sum(jnp.where(sel1, before, 0.0), axis=1, keepdims=True).astype(jnp.int32)
    rank2 = jnp.sum(jnp.where(sel2, before, 0.0), axis=1, keepdims=True).astype(jnp.int32)
    code_ref[...] = jnp.where(two == 0, i1 * n_tokens + rank1, i2 * n_tokens + rank2)
    total = carry_ref[...] + jnp.sum(chosen, axis=0, keepdims=True)
    carry_ref[...] = total
    cnt_ref[...] = total.astype(jnp.int32)


def rmsnorm_router(x, g, router_w, router_b, tr=256):
    n, d = x.shape
    return pl.pallas_call(
        functools.partial(_rmsnorm_router_kernel, n_tokens=n),
        grid=(n // tr,),
        in_specs=[pl.BlockSpec((tr, d), lambda i: (i, 0)),
                  pl.BlockSpec((1, d), lambda i: (0, 0)),
                  pl.BlockSpec((d, N_EXPERTS), lambda i: (0, 0)),
                  pl.BlockSpec((1, N_EXPERTS), lambda i: (0, 0))],
        out_specs=[pl.BlockSpec((tr, d // 2), lambda i: (i, 0)),
                   pl.BlockSpec((tr, TOP_K), lambda i: (i, 0)),
                   pl.BlockSpec((tr, TOP_K), lambda i: (i, 0)),
                   pl.BlockSpec((1, N_EXPERTS), lambda i: (0, 0))],
        out_shape=[jax.ShapeDtypeStruct((n, d // 2), jnp.uint32),
                   jax.ShapeDtypeStruct((n, TOP_K), jnp.int32),
                   jax.ShapeDtypeStruct((n, TOP_K), F32),
                   jax.ShapeDtypeStruct((1, N_EXPERTS), jnp.int32)],
        scratch_shapes=[pltpu.VMEM((1, N_EXPERTS), F32)],
        compiler_params=_params("arbitrary"),
        name="rmsnorm_router",
    )(x, g.reshape(1, d), router_w, router_b.reshape(1, N_EXPERTS))


def _row_copy(src_ref, src_row, dst_ref, dst_row, sem, rows=1):
    return pltpu.make_async_copy(src_ref.at[pl.ds(src_row, rows)],
                                 dst_ref.at[pl.ds(dst_row, rows)], sem)


def _dispatch_kernel(code_ref, cnt_ref, src_ref, dst_ref, slot_ref, tile_e_ref, nused_ref,
                     off_ref, sem, *, n_tokens, n_tiles):
    step = pl.program_id(0)
    tr = src_ref.shape[0]
    shift = n_tokens.bit_length() - 1
    round_up = lambda v: ((v + MOE_TILE - 1) // MOE_TILE) * MOE_TILE

    @pl.when(step == 0)
    def _():
        off_ref[0] = 0
        for e in range(N_EXPERTS):
            off_ref[e + 1] = off_ref[e] + round_up(cnt_ref[e])
        n_used = off_ref[N_EXPERTS] // MOE_TILE
        nused_ref[0] = n_used

        def tile_body(i, carry):
            row = jnp.minimum(i, n_used - 1) * MOE_TILE
            e = jnp.int32(0)
            for k in range(1, N_EXPERTS):
                e = e + (row >= off_ref[k]).astype(jnp.int32)
            tile_e_ref[i] = e
            return carry
        lax.fori_loop(0, n_tiles, tile_body, 0)

    def issue(k, carry):
        for kk in range(TOP_K):
            i = TOP_K * (step * tr + k) + kk
            code = code_ref[i]
            slot = off_ref[code >> shift] + (code & (n_tokens - 1))
            slot_ref[i] = slot
            _row_copy(src_ref, k, dst_ref, slot, sem).start()
        return carry
    lax.fori_loop(0, tr, issue, 0, unroll=4)
    _row_copy(dst_ref, 0, dst_ref, 0, sem, TOP_K * tr).wait()

    @pl.when(step == pl.num_programs(0) - 1)
    def _():
        def fill(lo, hi):
            def pad_start(row, carry):
                _row_copy(src_ref, 0, dst_ref, row, sem).start()
                return carry
            lax.fori_loop(lo, hi, pad_start, 0)

            def pad_wait(row, carry):
                _row_copy(src_ref, 0, dst_ref, 0, sem).wait()
                return carry
            lax.fori_loop(lo, hi, pad_wait, 0)

        for e in range(N_EXPERTS):
            fill(off_ref[e] + cnt_ref[e], off_ref[e + 1])
        fill(off_ref[N_EXPERTS], n_tiles * MOE_TILE)


def moe_dispatch(codes, counts, rows, tr=256):
    n, width = rows.shape
    assert n & (n - 1) == 0
    n_tiles = (TOP_K * n) // MOE_TILE + N_EXPERTS
    smem = pl.BlockSpec(memory_space=pltpu.SMEM)
    return pl.pallas_call(
        functools.partial(_dispatch_kernel, n_tokens=n, n_tiles=n_tiles),
        grid_spec=pltpu.PrefetchScalarGridSpec(
            num_scalar_prefetch=2, grid=(n // tr,),
            in_specs=[pl.BlockSpec((tr, width), lambda i, code, cnt: (i, 0))],
            out_specs=[pl.BlockSpec(memory_space=pl.ANY), smem, smem, smem],
            scratch_shapes=[pltpu.SMEM((N_EXPERTS + 1,), jnp.int32),
                            pltpu.SemaphoreType.DMA(())]),
        out_shape=[jax.ShapeDtypeStruct((n_tiles * MOE_TILE, width), rows.dtype),
                   jax.ShapeDtypeStruct((TOP_K * n,), jnp.int32),
                   jax.ShapeDtypeStruct((n_tiles,), jnp.int32),
                   jax.ShapeDtypeStruct((1,), jnp.int32)],
        compiler_params=_params("arbitrary"),
        name="moe_dispatch",
    )(codes.reshape(TOP_K * n), counts.reshape(N_EXPERTS), rows)


def _moe_up_kernel(tile_e_ref, nused_ref, x_ref, wg_ref, wu_ref, o_ref):
    del tile_e_ref

    @pl.when(pl.program_id(1) < nused_ref[0])
    def _():
        half = x_ref.shape[1]
        hi, lo = _unpack_bf16_pairs(x_ref[...])
        hi = hi.astype(BF16)
        lo = lo.astype(BF16)
        gate = (jnp.dot(hi, wg_ref[:half, :], preferred_element_type=F32)
                + jnp.dot(lo, wg_ref[half:, :], preferred_element_type=F32))
        up = (jnp.dot(hi, wu_ref[:half, :], preferred_element_type=F32)
              + jnp.dot(lo, wu_ref[half:, :], preferred_element_type=F32))
        o_ref[...] = (jax.nn.silu(gate) * up).astype(o_ref.dtype)

    @pl.when(pl.program_id(1) >= nused_ref[0])
    def _():
        o_ref[...] = jnp.zeros_like(o_ref)


def moe_up(xs, tile_e, n_used, w_gate, w_up, layer, tn=512):
    p, half = xs.shape
    _, n_exp, k, f = w_gate.shape
    n_tiles = p // MOE_TILE
    row_tile = lambda j, i, te, nu: (jnp.minimum(i, nu[0] - 1), 0)
    w_spec = pl.BlockSpec((None, None, k, tn), lambda j, i, te, nu: (layer, te[i], 0, j))
    return pl.pallas_call(
        _moe_up_kernel,
        grid_spec=pltpu.PrefetchScalarGridSpec(
            num_scalar_prefetch=2, grid=(f // tn, n_tiles),
            in_specs=[pl.BlockSpec((MOE_TILE, half), row_tile), w_spec, w_spec],
            out_specs=pl.BlockSpec((MOE_TILE, tn), lambda j, i, te, nu: (i, j))),
        out_shape=jax.ShapeDtypeStruct((p, f), BF16),
        compiler_params=_params("arbitrary", "arbitrary"),
        name="moe_up",
    )(tile_e, n_used, xs, w_gate, w_up)


def _moe_down_kernel(tile_e_ref, nused_ref, a_ref, w_ref, o_ref):
    del tile_e_ref

    @pl.when(pl.program_id(0) < nused_ref[0])
    def _():
        y = jnp.dot(a_ref[...], w_ref[...], preferred_element_type=F32)
        o_ref[...] = _pack_bf16_pairs(y)

    @pl.when(pl.program_id(0) >= nused_ref[0])
    def _():
        o_ref[...] = jnp.zeros_like(o_ref)


def moe_down(act, tile_e, n_used, w_down, layer):
    p, f = act.shape
    d = w_down.shape[3]
    row_tile = lambda i, te, nu: (jnp.minimum(i, nu[0] - 1), 0)
    return pl.pallas_call(
        _moe_down_kernel,
        grid_spec=pltpu.PrefetchScalarGridSpec(
            num_scalar_prefetch=2, grid=(p // MOE_TILE,),
            in_specs=[pl.BlockSpec((MOE_TILE, f), row_tile),
                      pl.BlockSpec((None, None, f, d), lambda i, te, nu: (layer, te[i], 0, 0))],
            out_specs=pl.BlockSpec((MOE_TILE, d // 2), lambda i, te, nu: (i, 0))),
        out_shape=jax.ShapeDtypeStruct((p, d // 2), jnp.uint32),
        compiler_params=_params("arbitrary"),
        name="moe_down",
    )(tile_e, n_used, act, w_down)


def _moe_combine_kernel(slot_ref, h_ref, p_ref, g_ref, ys_ref, h_out_ref, n_out_ref, buf_ref,
                        sems):
    i = pl.program_id(0)
    tr = h_ref.shape[0]
    half = ys_ref.shape[1]

    def start_gather(tile, b):
        def issue(k, carry):
            for kk in range(TOP_K):
                slot = slot_ref[TOP_K * (tile * tr + k) + kk]
                _row_copy(ys_ref, slot, buf_ref.at[b, kk], k, sems.at[b]).start()
            return carry
        lax.fori_loop(0, tr, issue, 0, unroll=4)

    @pl.when(i == 0)
    def _():
        start_gather(0, 0)

    @pl.when(i + 1 < pl.num_programs(0))
    def _():
        start_gather(i + 1, (i + 1) % 2)

    b = i % 2
    for kk in range(TOP_K):
        _row_copy(ys_ref, 0, buf_ref.at[b, kk], 0, sems.at[b], tr).wait()

    p1 = p_ref[:, 0:1]
    p2 = p_ref[:, 1:2]
    hi1, lo1 = _unpack_bf16_pairs(buf_ref[b, 0])
    hi2, lo2 = _unpack_bf16_pairs(buf_ref[b, 1])
    left = h_ref[:, :half] + (p1 * hi1 + p2 * hi2)
    right = h_ref[:, half:] + (p1 * lo1 + p2 * lo2)
    if h_out_ref is not None:
        h_out_ref[:, :half] = left
        h_out_ref[:, half:] = right
    if n_out_ref is not None:
        ms = (jnp.sum(left * left, axis=-1, keepdims=True)
              + jnp.sum(right * right, axis=-1, keepdims=True)) / (2 * half)
        scale = lax.rsqrt(ms + EPS)
        n_out_ref[:, :half] = (left * scale * g_ref[:, :half]).astype(n_out_ref.dtype)
        n_out_ref[:, half:] = (right * scale * g_ref[:, half:]).astype(n_out_ref.dtype)


def moe_combine(h, slots, ys, probs, next_g, final, tr=256):
    n, d = h.shape
    width = ys.shape[1]
    row_block = pl.BlockSpec((tr, d), lambda i, s: (i, 0))
    norm_shape = jax.ShapeDtypeStruct((n, d), F32 if final else BF16)
    out_specs = [row_block] if final else [row_block, row_block]
    out_shape = [norm_shape] if final else [jax.ShapeDtypeStruct((n, d), F32), norm_shape]

    def body(slot_ref, h_ref, p_ref, g_ref, ys_ref, *rest):
        outs, (buf_ref, sems) = rest[:-2], rest[-2:]
        h_out_ref, n_out_ref = (None, outs[0]) if final else outs
        _moe_combine_kernel(slot_ref, h_ref, p_ref, g_ref, ys_ref, h_out_ref, n_out_ref,
                            buf_ref, sems)

    outs = pl.pallas_call(
        body,
        grid_spec=pltpu.PrefetchScalarGridSpec(
            num_scalar_prefetch=1, grid=(n // tr,),
            in_specs=[row_block,
                      pl.BlockSpec((tr, TOP_K), lambda i, s: (i, 0)),
                      pl.BlockSpec((1, d), lambda i, s: (0, 0)),
                      pl.BlockSpec(memory_space=pl.ANY)],
            out_specs=out_specs,
            scratch_shapes=[pltpu.VMEM((2, TOP_K, tr, width), ys.dtype),
                            pltpu.SemaphoreType.DMA((2,))]),
        out_shape=out_shape,
        input_output_aliases={1: 0},
        compiler_params=_params("arbitrary"),
        name="moe_combine",
    )(slots, h, probs, next_g.reshape(1, d), ys)
    return (None, outs[0]) if final else outs


def moe_ffn(h, norm_g, router_w, router_b, w_gate, w_up, w_down, layer, next_g, final):
    rows, codes, probs, counts = rmsnorm_router(h, norm_g, router_w, router_b)
    xs, slots, tile_e, n_used = moe_dispatch(codes, counts, rows)
    act = moe_up(xs, tile_e, n_used, w_gate, w_up, layer)
    ys = moe_down(act, tile_e, n_used, w_down, layer)
    return moe_combine(h, slots, ys, probs, next_g, final)


def _matmul_cast_weight_kernel(x_ref, w_ref, o_ref, wb_ref):
    @pl.when(pl.program_id(1) == 0)
    def _():
        wb_ref[...] = w_ref[...].astype(BF16)

    o_ref[...] = jnp.dot(x_ref[...], wb_ref[...],
                         preferred_element_type=F32).astype(o_ref.dtype)


def matmul_cols(x, w, layer, n_cols, out_dtype, tm=1024, tn=768):
    m, k = x.shape
    return pl.pallas_call(
        _matmul_cast_weight_kernel,
        grid=(n_cols // tn, m // tm),
        in_specs=[pl.BlockSpec((tm, k), lambda j, i: (i, 0)),
                  pl.BlockSpec((None, k, tn), lambda j, i: (layer, 0, j))],
        out_specs=pl.BlockSpec((tm, tn), lambda j, i: (i, j)),
        out_shape=jax.ShapeDtypeStruct((m, n_cols), out_dtype),
        scratch_shapes=[pltpu.VMEM((k, tn), BF16)],
        compiler_params=_params("arbitrary", "arbitrary"),
        name="in_proj",
    )(x, w)


def _matmul_residual_kernel(x_ref, w_ref, r_ref, o_ref):
    o_ref[...] = r_ref[...] + jnp.dot(x_ref[...], w_ref[...], preferred_element_type=F32)


def matmul_residual(x, w, layer, res, tm, tn, in_place=True):
    m, k = x.shape
    n = w.shape[2]
    return pl.pallas_call(
        _matmul_residual_kernel,
        grid=(n // tn, m // tm),
        in_specs=[pl.BlockSpec((tm, k), lambda j, i: (i, 0)),
                  pl.BlockSpec((None, k, tn), lambda j, i: (layer, 0, j)),
                  pl.BlockSpec((tm, tn), lambda j, i: (i, j))],
        out_specs=pl.BlockSpec((tm, tn), lambda j, i: (i, j)),
        out_shape=jax.ShapeDtypeStruct((m, n), F32),
        input_output_aliases={2: 0} if in_place else {},
        compiler_params=_params("arbitrary", "arbitrary"),
        name="matmul_residual",
    )(x, w, res)


def _gelu(x):
    return 0.5 * x * (1.0 + lax.erf(x * (2.0 ** -0.5)))


def _shift_rows(x, k):
    return pltpu.roll(x, k, axis=0)


def _mixer_kernel(p_ref, poolw_ref, pscale_ref, sgunorm_ref, ws_ref, sb_ref, convw_ref,
                  y_ref, halo_ref, *, ts):
    w = BRANCH_WIDTH
    s_idx = pl.program_id(1)

    @pl.when(s_idx == 0)
    def _():
        halo_ref[...] = jnp.zeros_like(halo_ref)

    a = p_ref[:, 0:w].astype(F32)
    ext_a = jnp.concatenate([halo_ref[:, 0:w], a], axis=0)
    pos = s_idx * ts + lax.broadcasted_iota(jnp.int32, (ts, 1), 0)
    for g, win in enumerate(POOL_WINDOWS):
        cols = slice(g * POOL_GROUP_DIM, (g + 1) * POOL_GROUP_DIM)
        acc = ext_a[:, cols]
        step = 1
        while step < win:
            acc = acc + _shift_rows(acc, step)
            step *= 2
        cnt = jnp.minimum(pos + 1, win).astype(F32)
        pooled = acc[HALO_ROWS:, :] / cnt - a[:, cols]
        ya = jnp.dot(pooled.astype(BF16), poolw_ref[g], preferred_element_type=F32)
        y_ref[:, cols] = (ya * pscale_ref[:, cols]).astype(y_ref.dtype)

    u = _gelu(p_ref[:, w:2 * w].astype(F32))
    v = _rms(_gelu(p_ref[:, 2 * w:3 * w].astype(F32)), sgunorm_ref[...]).astype(BF16)
    row = lax.broadcasted_iota(jnp.int32, (SGU_CHUNK, SGU_CHUNK), 0)
    col = lax.broadcasted_iota(jnp.int32, (SGU_CHUNK, SGU_CHUNK), 1)
    for g in range(SGU_HEADS):
        cols = slice(g * SGU_HEAD_DIM, (g + 1) * SGU_HEAD_DIM)
        w_causal = jnp.where(row >= col, ws_ref[g], 0.0).astype(BF16)
        bias = sb_ref[:, g:g + 1]
        for c in range(ts // SGU_CHUNK):
            rows = slice(c * SGU_CHUNK, (c + 1) * SGU_CHUNK)
            vm = jnp.dot(w_causal, v[rows, cols], preferred_element_type=F32) + bias
            y_ref[rows, w + g * SGU_HEAD_DIM:w + (g + 1) * SGU_HEAD_DIM] = (
                u[rows, cols] * vm).astype(y_ref.dtype)

    hc = p_ref[:, 5 * w:6 * w].astype(F32) * p_ref[:, 3 * w:4 * w].astype(F32)
    ext_h = jnp.concatenate([halo_ref[:, w:2 * w], hc], axis=0)
    yc = (convw_ref[2:3, :] * ext_h
          + convw_ref[1:2, :] * _shift_rows(ext_h, 1)
          + convw_ref[0:1, :] * _shift_rows(ext_h, 2))
    y_ref[:, 2 * w:3 * w] = (p_ref[:, 4 * w:5 * w].astype(F32)
                             * yc[HALO_ROWS:, :]).astype(y_ref.dtype)

    halo_ref[:, 0:w] = a[ts - HALO_ROWS:, :]
    halo_ref[:, w:2 * w] = hc[ts - HALO_ROWS:, :]


def token_mixers(p6, pool_w, pool_scale, sgu_norm, sgu_ws, sgu_b, conv_w, bsz, seq, ts=256):
    n = p6.shape[0]
    w = BRANCH_WIDTH
    tiles = seq // ts
    row_block = lambda b, s: (b * tiles + s, 0)
    const2 = lambda b, s: (0, 0)
    const3 = lambda b, s: (0, 0, 0)
    return pl.pallas_call(
        functools.partial(_mixer_kernel, ts=ts),
        grid=(bsz, tiles),
        in_specs=[pl.BlockSpec((ts, MIX_COLS), row_block),
                  pl.BlockSpec((POOL_GROUPS, POOL_GROUP_DIM, POOL_GROUP_DIM), const3),
                  pl.BlockSpec((1, w), const2),
                  pl.BlockSpec((1, w), const2),
                  pl.BlockSpec((SGU_HEADS, SGU_CHUNK, SGU_CHUNK), const3),
                  pl.BlockSpec((SGU_CHUNK, SGU_HEADS), const2),
                  pl.BlockSpec((CONV_WIDTH, w), const2)],
        out_specs=pl.BlockSpec((ts, N_BRANCHES * w), row_block),
        out_shape=jax.ShapeDtypeStruct((n, N_BRANCHES * w), BF16),
        scratch_shapes=[pltpu.VMEM((HALO_ROWS, 2 * w), F32)],
        compiler_params=_params("arbitrary", "arbitrary"),
    )(p6, pool_w.astype(BF16), pool_scale.reshape(1, w), sgu_norm.reshape(1, w),
      sgu_ws, sgu_b.T, conv_w)


LANES = 128


def _gate_merge_kernel(xn_ref, y_ref, wga_ref, wgb_ref, wgc_ref, wb_ref, *rest, round_blocks):
    w = BRANCH_WIDTH
    o_ref = rest[-2] if round_blocks else rest[-1]
    xn = xn_ref[...]
    acc = None
    for n, wg_ref in enumerate((wga_ref, wgb_ref, wgc_ref)):
        gate = jnp.dot(xn, wg_ref[...], preferred_element_type=F32)
        branch = jnp.dot(y_ref[:, n * w:(n + 1) * w], wb_ref[n], preferred_element_type=F32)
        term = jax.nn.sigmoid(gate) * branch
        acc = term if acc is None else acc + term
    o_ref[...] = acc.astype(o_ref.dtype)

    if round_blocks:
        next_w_ref, next_o_ref = rest[0], rest[-1]

        @pl.when(pl.program_id(1) < round_blocks)
        def _():
            next_o_ref[...] = next_w_ref[...].astype(BF16)


def gate_merge(xn, y, w_gates, w_branch, layer, w_in=None, tm=512, tn=512):
    m, d = xn.shape
    w = BRANCH_WIDTH
    gate_stride = D_MODEL // tn
    n_col, n_row = D_MODEL // tn, m // tm
    gate_cols = N_BRANCHES * D_MODEL

    def gate_spec(n):
        return pl.BlockSpec((d, tn), lambda j, i: (0, n * gate_stride + j))

    in_specs = [pl.BlockSpec((tm, d), lambda j, i: (i, 0)),
                pl.BlockSpec((tm, N_BRANCHES * w), lambda j, i: (i, 0)),
                gate_spec(0), gate_spec(1), gate_spec(2),
                pl.BlockSpec((None, N_BRANCHES, w, tn), lambda j, i: (layer, 0, 0, j))]
    out_specs = [pl.BlockSpec((tm, tn), lambda j, i: (i, j))]
    out_shape = [jax.ShapeDtypeStruct((m, D_MODEL), BF16)]
    operands = [xn, y, w_gates, w_gates, w_gates, w_branch]
    round_blocks = 0
    if w_in is not None:
        round_blocks = gate_cols // LANES // n_col
        assert round_blocks * n_col * LANES == gate_cols and round_blocks <= n_row
        block = lambda j, i: j * round_blocks + jnp.minimum(i, round_blocks - 1)
        first = MIX_COLS // LANES
        in_specs.append(pl.BlockSpec((None, d, LANES),
                                     lambda j, i: (layer + 1, 0, first + block(j, i))))
        out_specs.append(pl.BlockSpec((d, LANES), lambda j, i: (0, block(j, i))))
        out_shape.append(jax.ShapeDtypeStruct((d, gate_cols), BF16))
        operands.append(w_in)
    outs = pl.pallas_call(
        functools.partial(_gate_merge_kernel, round_blocks=round_blocks),
        grid=(n_col, n_row),
        in_specs=in_specs,
        out_specs=out_specs,
        out_shape=out_shape,
        compiler_params=_params("arbitrary", "arbitrary"),
        name="gate_merge",
    )(*operands)
    return outs if round_blocks else (outs[0], None)


ROUND_STEPS = 128


def _swiglu_up_kernel(x_ref, wg_ref, wu_ref, *rest, n_round):
    round_in, o_ref, round_out = rest[:n_round], rest[n_round], rest[n_round + 1:]
    x = x_ref[...]
    gate = jnp.dot(x, wg_ref[...], preferred_element_type=F32)
    up = jnp.dot(x, wu_ref[...], preferred_element_type=F32)
    o_ref[...] = (jax.nn.silu(gate) * up).astype(o_ref.dtype)

    if n_round:
        step = pl.program_id(0) * pl.num_programs(1) + pl.program_id(1)

        @pl.when(step < ROUND_STEPS)
        def _():
            for src_ref, dst_ref in zip(round_in, round_out):
                dst_ref[...] = src_ref[...].astype(BF16)


def swiglu_up(x, w_gate, w_up, layer, round_weights=(), tm=1024, tn=512):
    m, k = x.shape
    n = w_gate.shape[2]
    n_col, n_row = n // tn, m // tm
    assert n_col * n_row >= ROUND_STEPS
    w_spec = pl.BlockSpec((None, k, tn), lambda j, i: (layer, 0, j))
    slab = lambda j, i: jnp.minimum(j * n_row + i, ROUND_STEPS - 1)
    in_specs = [pl.BlockSpec((tm, k), lambda j, i: (i, 0)), w_spec, w_spec]
    out_specs = [pl.BlockSpec((tm, tn), lambda j, i: (i, j))]
    out_shape = [jax.ShapeDtypeStruct((m, n), BF16)]
    operands = [x, w_gate, w_up]
    for w in round_weights:
        cols = w.shape[-1]
        rows = w[0].size // cols
        slab_rows = rows // ROUND_STEPS
        assert slab_rows * ROUND_STEPS == rows and slab_rows % 16 == 0
        in_specs.append(pl.BlockSpec((slab_rows, cols),
                                     lambda j, i: (layer * ROUND_STEPS + slab(j, i), 0)))
        out_specs.append(pl.BlockSpec((slab_rows, cols), lambda j, i: (slab(j, i), 0)))
        out_shape.append(jax.ShapeDtypeStruct((rows, cols), BF16))
        operands.append(w.reshape(w.shape[0] * rows, cols))
    outs = pl.pallas_call(
        functools.partial(_swiglu_up_kernel, n_round=len(round_weights)),
        grid=(n_col, n_row),
        in_specs=in_specs,
        out_specs=out_specs,
        out_shape=out_shape,
        compiler_params=_params("arbitrary", "arbitrary"),
        name="swiglu_up",
    )(*operands)
    return outs[0], outs[1:]


def kernel(x, norm_mix, w_in, pool_w, pool_scale, sgu_norm, sgu_ws, sgu_b, conv_w, w_branch, w_out, norm_ffn, dense_w_gate, dense_w_up, dense_w_down, router_w, router_b, moe_w_gate, moe_w_up, moe_w_down, norm_final):
    bsz, seq, d = x.shape
    h = x.reshape(bsz * seq, d)
    w_gates = w_in[0, :, MIX_COLS:].astype(BF16)
    w_branch = w_branch.astype(BF16)
    w_out = w_out.astype(BF16)
    dense_w_gate = dense_w_gate.astype(BF16)
    dense_w_up = dense_w_up.astype(BF16)
    dense_w_down = dense_w_down.astype(BF16)
    expert_w = None
    xn = None
    for layer in range(DEPTH):
        last = layer == DEPTH - 1
        if xn is None:
            xn = rmsnorm(h, norm_mix[layer], BF16)
        p6 = matmul_cols(xn, w_in, layer, MIX_COLS, BF16)
        y = token_mixers(p6, pool_w[layer], pool_scale[layer], sgu_norm[layer], sgu_ws[layer],
                         sgu_b[layer], conv_w[layer], bsz, seq)
        merged, w_gates = gate_merge(xn, y, w_gates, w_branch, layer,
                                     w_in=None if last else w_in)
        h = matmul_residual(merged, w_out, layer, h, tm=1024, tn=1024, in_place=layer > 0)
        xn = None
        j = layer // 2
        if layer % 2 == 0:
            hn = rmsnorm(h, norm_ffn[layer], BF16)
            stacks = (moe_w_gate, moe_w_up, moe_w_down) if layer + 1 < DEPTH else ()
            act, rounded = swiglu_up(hn, dense_w_gate, dense_w_up, j, round_weights=stacks)
            expert_w = [r.reshape((1,) + s.shape[1:]) for r, s in zip(rounded, stacks)]
            h = matmul_residual(act, dense_w_down, j, h, tm=512, tn=1024)
        else:
            next_g = norm_final if last else norm_mix[layer + 1]
            h, xn = moe_ffn(h, norm_ffn[layer], router_w[j], router_b[j],
                            *expert_w, 0, next_g, final=last)
    out = xn if DEPTH % 2 == 0 else rmsnorm(h, norm_final, F32)
    return out.reshape(bsz, seq, d)
```

```python
import functools

import jax
import jax.numpy as jnp
from jax import lax
from jax.experimental import pallas as pl
from jax.experimental.pallas import tpu as pltpu

D_MODEL = 4096
DEPTH = 4
BRANCH_WIDTH = D_MODEL // 4
N_BRANCHES = 3
POOL_WINDOWS = (2, 4, 8, 16)
POOL_GROUPS = 4
POOL_GROUP_DIM = BRANCH_WIDTH // POOL_GROUPS
SGU_CHUNK = 128
SGU_HEADS = 4
SGU_HEAD_DIM = BRANCH_WIDTH // SGU_HEADS
CONV_WIDTH = 3
MIX_COLS = 6 * BRANCH_WIDTH
D_FF = (3 * D_MODEL) // 2
N_EXPERTS = 8
TOP_K = 2
D_FF_EXPERT = D_FF // 4
EPS = 1e-6

V7X_VMEM_LIMIT_BYTES = 58 * 1024 * 1024
HALO_ROWS = 16

BF16 = jnp.bfloat16
F32 = jnp.float32


def _params(*semantics):
    return pltpu.CompilerParams(dimension_semantics=semantics,
                                vmem_limit_bytes=V7X_VMEM_LIMIT_BYTES)


def _rms(x, g):
    ms = jnp.mean(x * x, axis=-1, keepdims=True)
    return x * lax.rsqrt(ms + EPS) * g


def _rmsnorm_kernel(x_ref, g_ref, o_ref):
    o_ref[...] = _rms(x_ref[...], g_ref[...]).astype(o_ref.dtype)


def rmsnorm(x, g, out_dtype, tr=256):
    n, d = x.shape
    return pl.pallas_call(
        _rmsnorm_kernel,
        grid=(n // tr,),
        in_specs=[pl.BlockSpec((tr, d), lambda i: (i, 0)),
                  pl.BlockSpec((1, d), lambda i: (0, 0))],
        out_specs=pl.BlockSpec((tr, d), lambda i: (i, 0)),
        out_shape=jax.ShapeDtypeStruct((n, d), out_dtype),
        compiler_params=_params("arbitrary"),
    )(x, g.reshape(1, d))


MOE_TILE = 512
HIGH_HALF = 0xFFFF0000


def _pack_bf16_pairs(x):
    half = x.shape[1] // 2
    hi = lax.bitcast_convert_type(x[:, :half].astype(BF16).astype(F32), jnp.uint32)
    lo = lax.bitcast_convert_type(x[:, half:].astype(BF16).astype(F32), jnp.uint32)
    return (hi & jnp.uint32(HIGH_HALF)) | (lo >> 16)


def _unpack_bf16_pairs(p):
    hi = lax.bitcast_convert_type(p & jnp.uint32(HIGH_HALF), F32)
    lo = lax.bitcast_convert_type(p << 16, F32)
    return hi, lo


def _rmsnorm_router_kernel(x_ref, g_ref, rw_ref, rb_ref, o_ref, code_ref, prob_ref, cnt_ref,
                           carry_ref, *, n_tokens):
    i = pl.program_id(0)
    tr = x_ref.shape[0]

    @pl.when(i == 0)
    def _():
        carry_ref[...] = jnp.zeros_like(carry_ref)

    hn = _rms(x_ref[...], g_ref[...])
    o_ref[...] = _pack_bf16_pairs(hn)
    rw = rw_ref[...]
    hn_hi = hn.astype(BF16)
    hn_lo = (hn - hn_hi.astype(F32)).astype(BF16)
    rw_hi = rw.astype(BF16)
    rw_lo = (rw - rw_hi.astype(F32)).astype(BF16)
    logits = (jnp.dot(hn_hi, rw_hi, preferred_element_type=F32)
              + (jnp.dot(hn_hi, rw_lo, preferred_element_type=F32)
                 + jnp.dot(hn_lo, rw_hi, preferred_element_type=F32))) + rb_ref[...]
    idx = lax.broadcasted_iota(jnp.int32, logits.shape, 1)
    m1 = jnp.max(logits, axis=1, keepdims=True)
    i1 = jnp.min(jnp.where(logits == m1, idx, N_EXPERTS), axis=1, keepdims=True)
    sel1 = idx == i1
    rest = jnp.where(sel1, -jnp.inf, logits)
    m2 = jnp.max(rest, axis=1, keepdims=True)
    i2 = jnp.min(jnp.where((rest == m2) & jnp.logical_not(sel1), idx, N_EXPERTS),
                 axis=1, keepdims=True)
    sel2 = idx == i2
    e2 = jnp.exp(m2 - m1)
    denom = 1.0 + e2
    two = lax.broadcasted_iota(jnp.int32, (tr, TOP_K), 1)
    prob_ref[...] = jnp.where(two == 0, 1.0 / denom, e2 / denom)

    chosen = (sel1 | sel2).astype(F32)
    r = lax.broadcasted_iota(jnp.int32, (tr, tr), 0)
    c = lax.broadcasted_iota(jnp.int32, (tr, tr), 1)
    earlier = (c < r).astype(BF16)
    before = jnp.dot(earlier, chosen.astype(BF16), preferred_element_type=F32) + carry_ref[...]
    rank1 = jnp.sum(jnp.where(sel1, before, 0.0), axis=1, keepdims=True).astype(jnp.int32)
    rank2 = jnp.sum(jnp.where(sel2, before, 0.0), axis=1, keepdims=True).astype(jnp.int32)
    code_ref[...] = jnp.where(two == 0, i1 * n_tokens + rank1, i2 * n_tokens + rank2)
    total = carry_ref[...] + jnp.sum(chosen, axis=0, keepdims=True)
    carry_ref[...] = total
    cnt_ref[...] = total.astype(jnp.int32)


def rmsnorm_router(x, g, router_w, router_b, tr=256):
    n, d = x.shape
    return pl.pallas_call(
        functools.partial(_rmsnorm_router_kernel, n_tokens=n),
        grid=(n // tr,),
        in_specs=[pl.BlockSpec((tr, d), lambda i: (i, 0)),
                  pl.BlockSpec((1, d), lambda i: (0, 0)),
                  pl.BlockSpec((d, N_EXPERTS), lambda i: (0, 0)),
                  pl.BlockSpec((1, N_EXPERTS), lambda i: (0, 0))],
        out_specs=[pl.BlockSpec((tr, d // 2), lambda i: (i, 0)),
                   pl.BlockSpec((tr, TOP_K), lambda i: (i, 0)),
                   pl.BlockSpec((tr, TOP_K), lambda i: (i, 0)),
                   pl.BlockSpec((1, N_EXPERTS), lambda i: (0, 0))],
        out_shape=[jax.ShapeDtypeStruct((n, d // 2), jnp.uint32),
                   jax.ShapeDtypeStruct((n, TOP_K), jnp.int32),
                   jax.ShapeDtypeStruct((n, TOP_K), F32),
                   jax.ShapeDtypeStruct((1, N_EXPERTS), jnp.int32)],
        scratch_shapes=[pltpu.VMEM((1, N_EXPERTS), F32)],
        compiler_params=_params("arbitrary"),
        name="rmsnorm_router",
    )(x, g.reshape(1, d), router_w, router_b.reshape(1, N_EXPERTS))


def _row_copy(src_ref, src_row, dst_ref, dst_row, sem, rows=1):
    return pltpu.make_async_copy(src_ref.at[pl.ds(src_row, rows)],
                                 dst_ref.at[pl.ds(dst_row, rows)], sem)


def _dispatch_kernel(code_ref, cnt_ref, src_ref, dst_ref, slot_ref, tile_e_ref, nused_ref,
                     off_ref, sem, *, n_tokens, n_tiles):
    step = pl.program_id(0)
    tr = src_ref.shape[0]
    shift = n_tokens.bit_length() - 1
    round_up = lambda v: ((v + MOE_TILE - 1) // MOE_TILE) * MOE_TILE

    @pl.when(step == 0)
    def _():
        off_ref[0] = 0
        for e in range(N_EXPERTS):
            off_ref[e + 1] = off_ref[e] + round_up(cnt_ref[e])
        n_used = off_ref[N_EXPERTS] // MOE_TILE
        nused_ref[0] = n_used

        def tile_body(i, carry):
            row = jnp.minimum(i, n_used - 1) * MOE_TILE
            e = jnp.int32(0)
            for k in range(1, N_EXPERTS):
                e = e + (row >= off_ref[k]).astype(jnp.int32)
            tile_e_ref[i] = e
            return carry
        lax.fori_loop(0, n_tiles, tile_body, 0)

    def issue(k, carry):
        for kk in range(TOP_K):
            i = TOP_K * (step * tr + k) + kk
            code = code_ref[i]
            slot = off_ref[code >> shift] + (code & (n_tokens - 1))
            slot_ref[i] = slot
            _row_copy(src_ref, k, dst_ref, slot, sem).start()
        return carry
    lax.fori_loop(0, tr, issue, 0, unroll=4)
    _row_copy(dst_ref, 0, dst_ref, 0, sem, TOP_K * tr).wait()

    @pl.when(step == pl.num_programs(0) - 1)
    def _():
        def fill(lo, hi):
            def pad_start(row, carry):
                _row_copy(src_ref, 0, dst_ref, row, sem).start()
                return carry
            lax.fori_loop(lo, hi, pad_start, 0)

            def pad_wait(row, carry):
                _row_copy(src_ref, 0, dst_ref, 0, sem).wait()
                return carry
            lax.fori_loop(lo, hi, pad_wait, 0)

        for e in range(N_EXPERTS):
            fill(off_ref[e] + cnt_ref[e], off_ref[e + 1])
        fill(off_ref[N_EXPERTS], n_tiles * MOE_TILE)


def moe_dispatch(codes, counts, rows, tr=512):
    n, width = rows.shape
    assert n & (n - 1) == 0
    n_tiles = (TOP_K * n) // MOE_TILE + N_EXPERTS
    smem = pl.BlockSpec(memory_space=pltpu.SMEM)
    return pl.pallas_call(
        functools.partial(_dispatch_kernel, n_tokens=n, n_tiles=n_tiles),
        grid_spec=pltpu.PrefetchScalarGridSpec(
            num_scalar_prefetch=2, grid=(n // tr,),
            in_specs=[pl.BlockSpec((tr, width), lambda i, code, cnt: (i, 0))],
            out_specs=[pl.BlockSpec(memory_space=pl.ANY), smem, smem, smem],
            scratch_shapes=[pltpu.SMEM((N_EXPERTS + 1,), jnp.int32),
                            pltpu.SemaphoreType.DMA(())]),
        out_shape=[jax.ShapeDtypeStruct((n_tiles * MOE_TILE, width), rows.dtype),
                   jax.ShapeDtypeStruct((TOP_K * n,), jnp.int32),
                   jax.ShapeDtypeStruct((n_tiles,), jnp.int32),
                   jax.ShapeDtypeStruct((1,), jnp.int32)],
        compiler_params=_params("arbitrary"),
        name="moe_dispatch",
    )(codes.reshape(TOP_K * n), counts.reshape(N_EXPERTS), rows)


def _moe_up_kernel(tile_e_ref, nused_ref, x_ref, wg_ref, wu_ref, o_ref):
    del tile_e_ref

    @pl.when(pl.program_id(1) < nused_ref[0])
    def _():
        half = x_ref.shape[1]
        hi, lo = _unpack_bf16_pairs(x_ref[...])
        hi = hi.astype(BF16)
        lo = lo.astype(BF16)
        gate = (jnp.dot(hi, wg_ref[:half, :], preferred_element_type=F32)
                + jnp.dot(lo, wg_ref[half:, :], preferred_element_type=F32))
        up = (jnp.dot(hi, wu_ref[:half, :], preferred_element_type=F32)
              + jnp.dot(lo, wu_ref[half:, :], preferred_element_type=F32))
        o_ref[...] = (jax.nn.silu(gate) * up).astype(o_ref.dtype)

    @pl.when(pl.program_id(1) >= nused_ref[0])
    def _():
        o_ref[...] = jnp.zeros_like(o_ref)


def moe_up(xs, tile_e, n_used, w_gate, w_up, layer, tn=768):
    p, half = xs.shape
    _, n_exp, k, f = w_gate.shape
    n_tiles = p // MOE_TILE
    row_tile = lambda j, i, te, nu: (jnp.minimum(i, nu[0] - 1), 0)
    w_spec = pl.BlockSpec((None, None, k, tn), lambda j, i, te, nu: (layer, te[i], 0, j))
    return pl.pallas_call(
        _moe_up_kernel,
        grid_spec=pltpu.PrefetchScalarGridSpec(
            num_scalar_prefetch=2, grid=(f // tn, n_tiles),
            in_specs=[pl.BlockSpec((MOE_TILE, half), row_tile), w_spec, w_spec],
            out_specs=pl.BlockSpec((MOE_TILE, tn), lambda j, i, te, nu: (i, j))),
        out_shape=jax.ShapeDtypeStruct((p, f), BF16),
        compiler_params=_params("arbitrary", "arbitrary"),
        name="moe_up",
    )(tile_e, n_used, xs, w_gate, w_up)


def _moe_down_kernel(tile_e_ref, nused_ref, a_ref, w_ref, o_ref):
    del tile_e_ref

    @pl.when(pl.program_id(0) < nused_ref[0])
    def _():
        y = jnp.dot(a_ref[...], w_ref[...], preferred_element_type=F32)
        o_ref[...] = _pack_bf16_pairs(y)

    @pl.when(pl.program_id(0) >= nused_ref[0])
    def _():
        o_ref[...] = jnp.zeros_like(o_ref)


def moe_down(act, tile_e, n_used, w_down, layer):
    p, f = act.shape
    d = w_down.shape[3]
    row_tile = lambda i, te, nu: (jnp.minimum(i, nu[0] - 1), 0)
    return pl.pallas_call(
        _moe_down_kernel,
        grid_spec=pltpu.PrefetchScalarGridSpec(
            num_scalar_prefetch=2, grid=(p // MOE_TILE,),
            in_specs=[pl.BlockSpec((MOE_TILE, f), row_tile),
                      pl.BlockSpec((None, None, f, d), lambda i, te, nu: (layer, te[i], 0, 0))],
            out_specs=pl.BlockSpec((MOE_TILE, d // 2), lambda i, te, nu: (i, 0))),
        out_shape=jax.ShapeDtypeStruct((p, d // 2), jnp.uint32),
        compiler_params=_params("arbitrary"),
        name="moe_down",
    )(tile_e, n_used, act, w_down)


def _moe_combine_kernel(slot_ref, h_ref, p_ref, g_ref, ys_ref, h_out_ref, n_out_ref, buf_ref,
                        sems):
    i = pl.program_id(0)
    tr = h_ref.shape[0]
    half = ys_ref.shape[1]

    def start_gather(tile, b):
        def issue(k, carry):
            for kk in range(TOP_K):
                slot = slot_ref[TOP_K * (tile * tr + k) + kk]
                _row_copy(ys_ref, slot, buf_ref.at[b, kk], k, sems.at[b]).start()
            return carry
        lax.fori_loop(0, tr, issue, 0, unroll=4)

    @pl.when(i == 0)
    def _():
        start_gather(0, 0)

    @pl.when(i + 1 < pl.num_programs(0))
    def _():
        start_gather(i + 1, (i + 1) % 2)

    b = i % 2
    for kk in range(TOP_K):
        _row_copy(ys_ref, 0, buf_ref.at[b, kk], 0, sems.at[b], tr).wait()

    p1 = p_ref[:, 0:1]
    p2 = p_ref[:, 1:2]
    hi1, lo1 = _unpack_bf16_pairs(buf_ref[b, 0])
    hi2, lo2 = _unpack_bf16_pairs(buf_ref[b, 1])
    left = h_ref[:, :half] + (p1 * hi1 + p2 * hi2)
    right = h_ref[:, half:] + (p1 * lo1 + p2 * lo2)
    if h_out_ref is not None:
        h_out_ref[:, :half] = left
        h_out_ref[:, half:] = right
    if n_out_ref is not None:
        ms = (jnp.sum(left * left, axis=-1, keepdims=True)
              + jnp.sum(right * right, axis=-1, keepdims=True)) / (2 * half)
        scale = lax.rsqrt(ms + EPS)
        n_out_ref[:, :half] = (left * scale * g_ref[:, :half]).astype(n_out_ref.dtype)
        n_out_ref[:, half:] = (right * scale * g_ref[:, half:]).astype(n_out_ref.dtype)


def moe_combine(h, slots, ys, probs, next_g, final, tr=256):
    n, d = h.shape
    width = ys.shape[1]
    row_block = pl.BlockSpec((tr, d), lambda i, s: (i, 0))
    norm_shape = jax.ShapeDtypeStruct((n, d), F32 if final else BF16)
    out_specs = [row_block] if final else [row_block, row_block]
    out_shape = [norm_shape] if final else [jax.ShapeDtypeStruct((n, d), F32), norm_shape]

    def body(slot_ref, h_ref, p_ref, g_ref, ys_ref, *rest):
        outs, (buf_ref, sems) = rest[:-2], rest[-2:]
        h_out_ref, n_out_ref = (None, outs[0]) if final else outs
        _moe_combine_kernel(slot_ref, h_ref, p_ref, g_ref, ys_ref, h_out_ref, n_out_ref,
                            buf_ref, sems)

    outs = pl.pallas_call(
        body,
        grid_spec=pltpu.PrefetchScalarGridSpec(
            num_scalar_prefetch=1, grid=(n // tr,),
            in_specs=[row_block,
                      pl.BlockSpec((tr, TOP_K), lambda i, s: (i, 0)),
                      pl.BlockSpec((1, d), lambda i, s: (0, 0)),
                      pl.BlockSpec(memory_space=pl.ANY)],
            out_specs=out_specs,
            scratch_shapes=[pltpu.VMEM((2, TOP_K, tr, width), ys.dtype),
                            pltpu.SemaphoreType.DMA((2,))]),
        out_shape=out_shape,
        input_output_aliases={1: 0},
        compiler_params=_params("arbitrary"),
        name="moe_combine",
    )(slots, h, probs, next_g.reshape(1, d), ys)
    return (None, outs[0]) if final else outs


def moe_ffn(h, norm_g, router_w, router_b, w_gate, w_up, w_down, layer, next_g, final):
    rows, codes, probs, counts = rmsnorm_router(h, norm_g, router_w, router_b)
    xs, slots, tile_e, n_used = moe_dispatch(codes, counts, rows)
    act = moe_up(xs, tile_e, n_used, w_gate, w_up, layer)
    ys = moe_down(act, tile_e, n_used, w_down, layer)
    return moe_combine(h, slots, ys, probs, next_g, final)


def _matmul_cast_weight_kernel(x_ref, w_ref, o_ref, wb_ref):
    @pl.when(pl.program_id(1) == 0)
    def _():
        wb_ref[...] = w_ref[...].astype(BF16)

    o_ref[...] = jnp.dot(x_ref[...], wb_ref[...],
                         preferred_element_type=F32).astype(o_ref.dtype)


def matmul_cols(x, w, layer, n_cols, out_dtype, tm=1024, tn=768):
    m, k = x.shape
    return pl.pallas_call(
        _matmul_cast_weight_kernel,
        grid=(n_cols // tn, m // tm),
        in_specs=[pl.BlockSpec((tm, k), lambda j, i: (i, 0)),
                  pl.BlockSpec((None, k, tn), lambda j, i: (layer, 0, j))],
        out_specs=pl.BlockSpec((tm, tn), lambda j, i: (i, j)),
        out_shape=jax.ShapeDtypeStruct((m, n_cols), out_dtype),
        scratch_shapes=[pltpu.VMEM((k, tn), BF16)],
        compiler_params=_params("arbitrary", "arbitrary"),
        name="in_proj",
    )(x, w)


def _matmul_residual_kernel(x_ref, w_ref, r_ref, o_ref):
    o_ref[...] = r_ref[...] + jnp.dot(x_ref[...], w_ref[...], preferred_element_type=F32)


def matmul_residual(x, w, layer, res, tm, tn, in_place=True):
    m, k = x.shape
    n = w.shape[2]
    return pl.pallas_call(
        _matmul_residual_kernel,
        grid=(n // tn, m // tm),
        in_specs=[pl.BlockSpec((tm, k), lambda j, i: (i, 0)),
                  pl.BlockSpec((None, k, tn), lambda j, i: (layer, 0, j)),
                  pl.BlockSpec((tm, tn), lambda j, i: (i, j))],
        out_specs=pl.BlockSpec((tm, tn), lambda j, i: (i, j)),
        out_shape=jax.ShapeDtypeStruct((m, n), F32),
        input_output_aliases={2: 0} if in_place else {},
        compiler_params=_params("arbitrary", "arbitrary"),
        name="matmul_residual",
    )(x, w, res)


def _gelu(x):
    return 0.5 * x * (1.0 + lax.erf(x * (2.0 ** -0.5)))


def _shift_rows(x, k):
    return pltpu.roll(x, k, axis=0)


def _mixer_kernel(p_ref, poolw_ref, pscale_ref, sgunorm_ref, ws_ref, sb_ref, convw_ref,
                  y_ref, halo_ref, *, ts):
    w = BRANCH_WIDTH
    s_idx = pl.program_id(1)

    @pl.when(s_idx == 0)
    def _():
        halo_ref[...] = jnp.zeros_like(halo_ref)

    a = p_ref[:, 0:w].astype(F32)
    ext_a = jnp.concatenate([halo_ref[:, 0:w], a], axis=0)
    pos = s_idx * ts + lax.broadcasted_iota(jnp.int32, (ts, 1), 0)
    for g, win in enumerate(POOL_WINDOWS):
        cols = slice(g * POOL_GROUP_DIM, (g + 1) * POOL_GROUP_DIM)
        acc = ext_a[:, cols]
        step = 1
        while step < win:
            acc = acc + _shift_rows(acc, step)
            step *= 2
        cnt = jnp.minimum(pos + 1, win).astype(F32)
        pooled = acc[HALO_ROWS:, :] / cnt - a[:, cols]
        ya = jnp.dot(pooled.astype(BF16), poolw_ref[g], preferred_element_type=F32)
        y_ref[:, cols] = (ya * pscale_ref[:, cols]).astype(y_ref.dtype)

    u = _gelu(p_ref[:, w:2 * w].astype(F32))
    v = _rms(_gelu(p_ref[:, 2 * w:3 * w].astype(F32)), sgunorm_ref[...]).astype(BF16)
    row = lax.broadcasted_iota(jnp.int32, (SGU_CHUNK, SGU_CHUNK), 0)
    col = lax.broadcasted_iota(jnp.int32, (SGU_CHUNK, SGU_CHUNK), 1)
    for g in range(SGU_HEADS):
        cols = slice(g * SGU_HEAD_DIM, (g + 1) * SGU_HEAD_DIM)
        w_causal = jnp.where(row >= col, ws_ref[g], 0.0).astype(BF16)
        bias = sb_ref[:, g:g + 1]
        for c in range(ts // SGU_CHUNK):
            rows = slice(c * SGU_CHUNK, (c + 1) * SGU_CHUNK)
            vm = jnp.dot(w_causal, v[rows, cols], preferred_element_type=F32) + bias
            y_ref[rows, w + g * SGU_HEAD_DIM:w + (g + 1) * SGU_HEAD_DIM] = (
                u[rows, cols] * vm).astype(y_ref.dtype)

    hc = p_ref[:, 5 * w:6 * w].astype(F32) * p_ref[:, 3 * w:4 * w].astype(F32)
    ext_h = jnp.concatenate([halo_ref[:, w:2 * w], hc], axis=0)
    yc = (convw_ref[2:3, :] * ext_h
          + convw_ref[1:2, :] * _shift_rows(ext_h, 1)
          + convw_ref[0:1, :] * _shift_rows(ext_h, 2))
    y_ref[:, 2 * w:3 * w] = (p_ref[:, 4 * w:5 * w].astype(F32)
                             * yc[HALO_ROWS:, :]).astype(y_ref.dtype)

    halo_ref[:, 0:w] = a[ts - HALO_ROWS:, :]
    halo_ref[:, w:2 * w] = hc[ts - HALO_ROWS:, :]


def token_mixers(p6, pool_w, pool_scale, sgu_norm, sgu_ws, sgu_b, conv_w, bsz, seq, ts=256):
    n = p6.shape[0]
    w = BRANCH_WIDTH
    tiles = seq // ts
    row_block = lambda b, s: (b * tiles + s, 0)
    const2 = lambda b, s: (0, 0)
    const3 = lambda b, s: (0, 0, 0)
    return pl.pallas_call(
        functools.partial(_mixer_kernel, ts=ts),
        grid=(bsz, tiles),
        in_specs=[pl.BlockSpec((ts, MIX_COLS), row_block),
                  pl.BlockSpec((POOL_GROUPS, POOL_GROUP_DIM, POOL_GROUP_DIM), const3),
                  pl.BlockSpec((1, w), const2),
                  pl.BlockSpec((1, w), const2),
                  pl.BlockSpec((SGU_HEADS, SGU_CHUNK, SGU_CHUNK), const3),
                  pl.BlockSpec((SGU_CHUNK, SGU_HEADS), const2),
                  pl.BlockSpec((CONV_WIDTH, w), const2)],
        out_specs=pl.BlockSpec((ts, N_BRANCHES * w), row_block),
        out_shape=jax.ShapeDtypeStruct((n, N_BRANCHES * w), BF16),
        scratch_shapes=[pltpu.VMEM((HALO_ROWS, 2 * w), F32)],
        compiler_params=_params("arbitrary", "arbitrary"),
    )(p6, pool_w.astype(BF16), pool_scale.reshape(1, w), sgu_norm.reshape(1, w),
      sgu_ws, sgu_b.T, conv_w)


LANES = 128


def _gate_merge_kernel(xn_ref, y_ref, wga_ref, wgb_ref, wgc_ref, wb_ref, *rest, round_blocks):
    w = BRANCH_WIDTH
    o_ref = rest[-2] if round_blocks else rest[-1]
    xn = xn_ref[...]
    acc = None
    for n, wg_ref in enumerate((wga_ref, wgb_ref, wgc_ref)):
        gate = jnp.dot(xn, wg_ref[...], preferred_element_type=F32)
        branch = jnp.dot(y_ref[:, n * w:(n + 1) * w], wb_ref[n], preferred_element_type=F32)
        term = jax.nn.sigmoid(gate) * branch
        acc = term if acc is None else acc + term
    o_ref[...] = acc.astype(o_ref.dtype)

    if round_blocks:
        next_w_ref, next_o_ref = rest[0], rest[-1]

        @pl.when(pl.program_id(1) < round_blocks)
        def _():
            next_o_ref[...] = next_w_ref[...].astype(BF16)


def gate_merge(xn, y, w_gates, w_branch, layer, w_in=None, tm=512, tn=512):
    m, d = xn.shape
    w = BRANCH_WIDTH
    gate_stride = D_MODEL // tn
    n_col, n_row = D_MODEL // tn, m // tm
    gate_cols = N_BRANCHES * D_MODEL

    def gate_spec(n):
        return pl.BlockSpec((d, tn), lambda j, i: (0, n * gate_stride + j))

    in_specs = [pl.BlockSpec((tm, d), lambda j, i: (i, 0)),
                pl.BlockSpec((tm, N_BRANCHES * w), lambda j, i: (i, 0)),
                gate_spec(0), gate_spec(1), gate_spec(2),
                pl.BlockSpec((None, N_BRANCHES, w, tn), lambda j, i: (layer, 0, 0, j))]
    out_specs = [pl.BlockSpec((tm, tn), lambda j, i: (i, j))]
    out_shape = [jax.ShapeDtypeStruct((m, D_MODEL), BF16)]
    operands = [xn, y, w_gates, w_gates, w_gates, w_branch]
    round_blocks = 0
    if w_in is not None:
        round_blocks = gate_cols // LANES // n_col
        assert round_blocks * n_col * LANES == gate_cols and round_blocks <= n_row
        block = lambda j, i: j * round_blocks + jnp.minimum(i, round_blocks - 1)
        first = MIX_COLS // LANES
        in_specs.append(pl.BlockSpec((None, d, LANES),
                                     lambda j, i: (layer + 1, 0, first + block(j, i))))
        out_specs.append(pl.BlockSpec((d, LANES), lambda j, i: (0, block(j, i))))
        out_shape.append(jax.ShapeDtypeStruct((d, gate_cols), BF16))
        operands.append(w_in)
    outs = pl.pallas_call(
        functools.partial(_gate_merge_kernel, round_blocks=round_blocks),
        grid=(n_col, n_row),
        in_specs=in_specs,
        out_specs=out_specs,
        out_shape=out_shape,
        compiler_params=_params("arbitrary", "arbitrary"),
        name="gate_merge",
    )(*operands)
    return outs if round_blocks else (outs[0], None)


ROUND_STEPS = 128


def _swiglu_up_kernel(x_ref, wg_ref, wu_ref, *rest, n_round):
    round_in, o_ref, round_out = rest[:n_round], rest[n_round], rest[n_round + 1:]
    x = x_ref[...]
    gate = jnp.dot(x, wg_ref[...], preferred_element_type=F32)
    up = jnp.dot(x, wu_ref[...], preferred_element_type=F32)
    o_ref[...] = (jax.nn.silu(gate) * up).astype(o_ref.dtype)

    if n_round:
        step = pl.program_id(0) * pl.num_programs(1) + pl.program_id(1)

        @pl.when(step < ROUND_STEPS)
        def _():
            for src_ref, dst_ref in zip(round_in, round_out):
                dst_ref[...] = src_ref[...].astype(BF16)


def swiglu_up(x, w_gate, w_up, layer, round_weights=(), tm=1024, tn=512):
    m, k = x.shape
    n = w_gate.shape[2]
    n_col, n_row = n // tn, m // tm
    assert n_col * n_row >= ROUND_STEPS
    w_spec = pl.BlockSpec((None, k, tn), lambda j, i: (layer, 0, j))
    slab = lambda j, i: jnp.minimum(j * n_row + i, ROUND_STEPS - 1)
    in_specs = [pl.BlockSpec((tm, k), lambda j, i: (i, 0)), w_spec, w_spec]
    out_specs = [pl.BlockSpec((tm, tn), lambda j, i: (i, j))]
    out_shape = [jax.ShapeDtypeStruct((m, n), BF16)]
    operands = [x, w_gate, w_up]
    for w in round_weights:
        cols = w.shape[-1]
        rows = w[0].size // cols
        slab_rows = rows // ROUND_STEPS
        assert slab_rows * ROUND_STEPS == rows and slab_rows % 16 == 0
        in_specs.append(pl.BlockSpec((slab_rows, cols),
                                     lambda j, i: (layer * ROUND_STEPS + slab(j, i), 0)))
        out_specs.append(pl.BlockSpec((slab_rows, cols), lambda j, i: (slab(j, i), 0)))
        out_shape.append(jax.ShapeDtypeStruct((rows, cols), BF16))
        operands.append(w.reshape(w.shape[0] * rows, cols))
    outs = pl.pallas_call(
        functools.partial(_swiglu_up_kernel, n_round=len(round_weights)),
        grid=(n_col, n_row),
        in_specs=in_specs,
        out_specs=out_specs,
        out_shape=out_shape,
        compiler_params=_params("arbitrary", "arbitrary"),
        name="swiglu_up",
    )(*operands)
    return outs[0], outs[1:]


def kernel(x, norm_mix, w_in, pool_w, pool_scale, sgu_norm, sgu_ws, sgu_b, conv_w, w_branch, w_out, norm_ffn, dense_w_gate, dense_w_up, dense_w_down, router_w, router_b, moe_w_gate, moe_w_up, moe_w_down, norm_final):
    bsz, seq, d = x.shape
    h = x.reshape(bsz * seq, d)
    w_gates = w_in[0, :, MIX_COLS:].astype(BF16)
    w_branch = w_branch.astype(BF16)
    w_out = w_out.astype(BF16)
    dense_w_gate = dense_w_gate.astype(BF16)
    dense_w_up = dense_w_up.astype(BF16)
    dense_w_down = dense_w_down.astype(BF16)
    expert_w = None
    xn = None
    for layer in range(DEPTH):
        last = layer == DEPTH - 1
        if xn is None:
            xn = rmsnorm(h, norm_mix[layer], BF16)
        p6 = matmul_cols(xn, w_in, layer, MIX_COLS, BF16)
        y = token_mixers(p6, pool_w[layer], pool_scale[layer], sgu_norm[layer], sgu_ws[layer],
                         sgu_b[layer], conv_w[layer], bsz, seq)
        merged, w_gates = gate_merge(xn, y, w_gates, w_branch, layer,
                                     w_in=None if last else w_in)
        h = matmul_residual(merged, w_out, layer, h, tm=1024, tn=1024, in_place=layer > 0)
        xn = None
        j = layer // 2
        if layer % 2 == 0:
            hn = rmsnorm(h, norm_ffn[layer], BF16)
            stacks = (moe_w_gate, moe_w_up, moe_w_down) if layer + 1 < DEPTH else ()
            act, rounded = swiglu_up(hn, dense_w_gate, dense_w_up, j, round_weights=stacks)
            expert_w = [r.reshape((1,) + s.shape[1:]) for r, s in zip(rounded, stacks)]
            h = matmul_residual(act, dense_w_down, j, h, tm=512, tn=1024)
        else:
            next_g = norm_final if last else norm_mix[layer + 1]
            h, xn = moe_ffn(h, norm_ffn[layer], router_w[j], router_b[j],
                            *expert_w, 0, next_g, final=last)
    out = xn if DEPTH % 2 == 0 else rmsnorm(h, norm_final, F32)
    return out.reshape(bsz, seq, d)
```

```python
import functools

import jax
import jax.numpy as jnp
from jax import lax
from jax.experimental import pallas as pl
from jax.experimental.pallas import tpu as pltpu

D_MODEL = 4096
DEPTH = 4
BRANCH_WIDTH = D_MODEL // 4
N_BRANCHES = 3
POOL_WINDOWS = (2, 4, 8, 16)
POOL_GROUPS = 4
POOL_GROUP_DIM = BRANCH_WIDTH // POOL_GROUPS
SGU_CHUNK = 128
SGU_HEADS = 4
SGU_HEAD_DIM = BRANCH_WIDTH // SGU_HEADS
CONV_WIDTH = 3
MIX_COLS = 6 * BRANCH_WIDTH
D_FF = (3 * D_MODEL) // 2
N_EXPERTS = 8
TOP_K = 2
D_FF_EXPERT = D_FF // 4
EPS = 1e-6

V7X_VMEM_LIMIT_BYTES = 58 * 1024 * 1024
HALO_ROWS = 16

BF16 = jnp.bfloat16
F32 = jnp.float32


def _params(*semantics):
    return pltpu.CompilerParams(dimension_semantics=semantics,
                                vmem_limit_bytes=V7X_VMEM_LIMIT_BYTES)


def _rms(x, g):
    ms = jnp.mean(x * x, axis=-1, keepdims=True)
    return x * lax.rsqrt(ms + EPS) * g


def _rmsnorm_kernel(x_ref, g_ref, o_ref):
    o_ref[...] = _rms(x_ref[...], g_ref[...]).astype(o_ref.dtype)


def rmsnorm(x, g, out_dtype, tr=256):
    n, d = x.shape
    return pl.pallas_call(
        _rmsnorm_kernel,
        grid=(n // tr,),
        in_specs=[pl.BlockSpec((tr, d), lambda i: (i, 0)),
                  pl.BlockSpec((1, d), lambda i: (0, 0))],
        out_specs=pl.BlockSpec((tr, d), lambda i: (i, 0)),
        out_shape=jax.ShapeDtypeStruct((n, d), out_dtype),
        compiler_params=_params("arbitrary"),
    )(x, g.reshape(1, d))


MOE_TILE = 512
HIGH_HALF = 0xFFFF0000


def _pack_bf16_pairs(x):
    half = x.shape[1] // 2
    hi = lax.bitcast_convert_type(x[:, :half].astype(BF16).astype(F32), jnp.uint32)
    lo = lax.bitcast_convert_type(x[:, half:].astype(BF16).astype(F32), jnp.uint32)
    return (hi & jnp.uint32(HIGH_HALF)) | (lo >> 16)


def _unpack_bf16_pairs(p):
    hi = lax.bitcast_convert_type(p & jnp.uint32(HIGH_HALF), F32)
    lo = lax.bitcast_convert_type(p << 16, F32)
    return hi, lo


def _rmsnorm_router_kernel(x_ref, g_ref, rw_ref, rb_ref, o_ref, code_ref, prob_ref, cnt_ref,
                           carry_ref, *, n_tokens):
    i = pl.program_id(0)
    tr = x_ref.shape[0]

    @pl.when(i == 0)
    def _():
        carry_ref[...] = jnp.zeros_like(carry_ref)

    hn = _rms(x_ref[...], g_ref[...])
    o_ref[...] = _pack_bf16_pairs(hn)
    rw = rw_ref[...]
    hn_hi = hn.astype(BF16)
    hn_lo = (hn - hn_hi.astype(F32)).astype(BF16)
    rw_hi = rw.astype(BF16)
    rw_lo = (rw - rw_hi.astype(F32)).astype(BF16)
    logits = (jnp.dot(hn_hi, rw_hi, preferred_element_type=F32)
              + (jnp.dot(hn_hi, rw_lo, preferred_element_type=F32)
                 + jnp.dot(hn_lo, rw_hi, preferred_element_type=F32))) + rb_ref[...]
    idx = lax.broadcasted_iota(jnp.int32, logits.shape, 1)
    m1 = jnp.max(logits, axis=1, keepdims=True)
    i1 = jnp.min(jnp.where(logits == m1, idx, N_EXPERTS), axis=1, keepdims=True)
    sel1 = idx == i1
    rest = jnp.where(sel1, -jnp.inf, logits)
    m2 = jnp.max(rest, axis=1, keepdims=True)
    i2 = jnp.min(jnp.where((rest == m2) & jnp.logical_not(sel1), idx, N_EXPERTS),
                 axis=1, keepdims=True)
    sel2 = idx == i2
    e2 = jnp.exp(m2 - m1)
    denom = 1.0 + e2
    two = lax.broadcasted_iota(jnp.int32, (tr, TOP_K), 1)
    prob_ref[...] = jnp.where(two == 0, 1.0 / denom, e2 / denom)

    chosen = (sel1 | sel2).astype(F32)
    r = lax.broadcasted_iota(jnp.int32, (tr, tr), 0)
    c = lax.broadcasted_iota(jnp.int32, (tr, tr), 1)
    earlier = (c < r).astype(BF16)
    before = jnp.dot(earlier, chosen.astype(BF16), preferred_element_type=F32) + carry_ref[...]
    rank1 = jnp.sum(jnp.where(sel1, before, 0.0), axis=1, keepdims=True).astype(jnp.int32)
    rank2 = jnp.sum(jnp.where(sel2, before, 0.0), axis=1, keepdims=True).astype(jnp.int32)
    code_ref[...] = jnp.where(two == 0, i1 * n_tokens + rank1, i2 * n_tokens + rank2)
    total = carry_ref[...] + jnp.sum(chosen, axis=0, keepdims=True)
    carry_ref[...] = total
    cnt_ref[...] = total.astype(jnp.int32)


def rmsnorm_router(x, g, router_w, router_b, tr=256):
    n, d = x.shape
    return pl.pallas_call(
        functools.partial(_rmsnorm_router_kernel, n_tokens=n),
        grid=(n // tr,),
        in_specs=[pl.BlockSpec((tr, d), lambda i: (i, 0)),
                  pl.BlockSpec((1, d), lambda i: (0, 0)),
                  pl.BlockSpec((d, N_EXPERTS), lambda i: (0, 0)),
                  pl.BlockSpec((1, N_EXPERTS), lambda i: (0, 0))],
        out_specs=[pl.BlockSpec((tr, d // 2), lambda i: (i, 0)),
                   pl.BlockSpec((tr, TOP_K), lambda i: (i, 0)),
                   pl.BlockSpec((tr, TOP_K), lambda i: (i, 0)),
                   pl.BlockSpec((1, N_EXPERTS), lambda i: (0, 0))],
        out_shape=[jax.ShapeDtypeStruct((n, d // 2), jnp.uint32),
                   jax.ShapeDtypeStruct((n, TOP_K), jnp.int32),
                   jax.ShapeDtypeStruct((n, TOP_K), F32),
                   jax.ShapeDtypeStruct((1, N_EXPERTS), jnp.int32)],
        scratch_shapes=[pltpu.VMEM((1, N_EXPERTS), F32)],
        compiler_params=_params("arbitrary"),
        name="rmsnorm_router",
    )(x, g.reshape(1, d), router_w, router_b.reshape(1, N_EXPERTS))


def _row_copy(src_ref, src_row, dst_ref, dst_row, sem, rows=1):
    return pltpu.make_async_copy(src_ref.at[pl.ds(src_row, rows)],
                                 dst_ref.at[pl.ds(dst_row, rows)], sem)


def _dispatch_kernel(code_ref, cnt_ref, src_ref, dst_ref, slot_ref, tile_e_ref, nused_ref,
                     off_ref, sem, *, n_tokens, n_tiles):
    step = pl.program_id(0)
    tr = src_ref.shape[0]
    shift = n_tokens.bit_length() - 1
    round_up = lambda v: ((v + MOE_TILE - 1) // MOE_TILE) * MOE_TILE

    @pl.when(step == 0)
    def _():
        off_ref[0] = 0
        for e in range(N_EXPERTS):
            off_ref[e + 1] = off_ref[e] + round_up(cnt_ref[e])
        n_used = off_ref[N_EXPERTS] // MOE_TILE
        nused_ref[0] = n_used

        def tile_body(i, carry):
            row = jnp.minimum(i, n_used - 1) * MOE_TILE
            e = jnp.int32(0)
            for k in range(1, N_EXPERTS):
                e = e + (row >= off_ref[k]).astype(jnp.int32)
            tile_e_ref[i] = e
            return carry
        lax.fori_loop(0, n_tiles, tile_body, 0)

    def issue(k, carry):
        for kk in range(TOP_K):
            i = TOP_K * (step * tr + k) + kk
            code = code_ref[i]
            slot = off_ref[code >> shift] + (code & (n_tokens - 1))
            slot_ref[i] = slot
            _row_copy(src_ref, k, dst_ref, slot, sem).start()
        return carry
    lax.fori_loop(0, tr, issue, 0, unroll=4)
    _row_copy(dst_ref, 0, dst_ref, 0, sem, TOP_K * tr).wait()

    @pl.when(step == pl.num_programs(0) - 1)
    def _():
        def fill(lo, hi):
            def pad_start(row, carry):
                _row_copy(src_ref, 0, dst_ref, row, sem).start()
                return carry
            lax.fori_loop(lo, hi, pad_start, 0)

            def pad_wait(row, carry):
                _row_copy(src_ref, 0, dst_ref, 0, sem).wait()
                return carry
            lax.fori_loop(lo, hi, pad_wait, 0)

        for e in range(N_EXPERTS):
            fill(off_ref[e] + cnt_ref[e], off_ref[e + 1])
        fill(off_ref[N_EXPERTS], n_tiles * MOE_TILE)


def moe_dispatch(codes, counts, rows, tr=512):
    n, width = rows.shape
    assert n & (n - 1) == 0
    n_tiles = (TOP_K * n) // MOE_TILE + N_EXPERTS
    smem = pl.BlockSpec(memory_space=pltpu.SMEM)
    return pl.pallas_call(
        functools.partial(_dispatch_kernel, n_tokens=n, n_tiles=n_tiles),
        grid_spec=pltpu.PrefetchScalarGridSpec(
            num_scalar_prefetch=2, grid=(n // tr,),
            in_specs=[pl.BlockSpec((tr, width), lambda i, code, cnt: (i, 0))],
            out_specs=[pl.BlockSpec(memory_space=pl.ANY), smem, smem, smem],
            scratch_shapes=[pltpu.SMEM((N_EXPERTS + 1,), jnp.int32),
                            pltpu.SemaphoreType.DMA(())]),
        out_shape=[jax.ShapeDtypeStruct((n_tiles * MOE_TILE, width), rows.dtype),
                   jax.ShapeDtypeStruct((TOP_K * n,), jnp.int32),
                   jax.ShapeDtypeStruct((n_tiles,), jnp.int32),
                   jax.ShapeDtypeStruct((1,), jnp.int32)],
        compiler_params=_params("arbitrary"),
        name="moe_dispatch",
    )(codes.reshape(TOP_K * n), counts.reshape(N_EXPERTS), rows)


def _moe_up_kernel(tile_e_ref, nused_ref, x_ref, wg_ref, wu_ref, *rest, n_round):
    del tile_e_ref
    round_in, o_ref, round_out = rest[:n_round], rest[n_round], rest[n_round + 1:]

    if n_round:
        step = pl.program_id(0) * pl.num_programs(1) + pl.program_id(1)

        @pl.when(step < ROUND_STEPS)
        def _():
            for src_ref, dst_ref in zip(round_in, round_out):
                dst_ref[...] = src_ref[...].astype(BF16)

    @pl.when(pl.program_id(1) < nused_ref[0])
    def _():
        half = x_ref.shape[1]
        hi, lo = _unpack_bf16_pairs(x_ref[...])
        hi = hi.astype(BF16)
        lo = lo.astype(BF16)
        gate = (jnp.dot(hi, wg_ref[:half, :], preferred_element_type=F32)
                + jnp.dot(lo, wg_ref[half:, :], preferred_element_type=F32))
        up = (jnp.dot(hi, wu_ref[:half, :], preferred_element_type=F32)
              + jnp.dot(lo, wu_ref[half:, :], preferred_element_type=F32))
        o_ref[...] = (jax.nn.silu(gate) * up).astype(o_ref.dtype)

    @pl.when(pl.program_id(1) >= nused_ref[0])
    def _():
        o_ref[...] = jnp.zeros_like(o_ref)


def _round_side_job(pairs, slab):
    in_specs, out_specs, out_shape, operands = [], [], [], []
    for w, layer in pairs:
        cols = w.shape[-1]
        rows = w.size // (w.shape[0] * cols)
        slab_rows = rows // ROUND_STEPS
        assert slab_rows * ROUND_STEPS == rows and slab_rows % 16 == 0
        in_specs.append(pl.BlockSpec(
            (slab_rows, cols), lambda *a, layer=layer: (layer * ROUND_STEPS + slab(*a), 0)))
        out_specs.append(pl.BlockSpec((slab_rows, cols), lambda *a: (slab(*a), 0)))
        out_shape.append(jax.ShapeDtypeStruct((rows, cols), BF16))
        operands.append(w.reshape(w.shape[0] * rows, cols))
    return in_specs, out_specs, out_shape, operands


def moe_up(xs, tile_e, n_used, w_gate, w_up, layer, round_pairs=(), tn=768):
    p, half = xs.shape
    _, n_exp, k, f = w_gate.shape
    n_tiles = p // MOE_TILE
    assert not round_pairs or (f // tn) * n_tiles >= ROUND_STEPS
    row_tile = lambda j, i, te, nu: (jnp.minimum(i, nu[0] - 1), 0)
    w_spec = pl.BlockSpec((None, None, k, tn), lambda j, i, te, nu: (layer, te[i], 0, j))
    r_in, r_out, r_shape, r_ops = _round_side_job(
        round_pairs, lambda j, i, te, nu: jnp.minimum(j * n_tiles + i, ROUND_STEPS - 1))
    outs = pl.pallas_call(
        functools.partial(_moe_up_kernel, n_round=len(round_pairs)),
        grid_spec=pltpu.PrefetchScalarGridSpec(
            num_scalar_prefetch=2, grid=(f // tn, n_tiles),
            in_specs=[pl.BlockSpec((MOE_TILE, half), row_tile), w_spec, w_spec] + r_in,
            out_specs=[pl.BlockSpec((MOE_TILE, tn), lambda j, i, te, nu: (i, j))] + r_out),
        out_shape=[jax.ShapeDtypeStruct((p, f), BF16)] + r_shape,
        compiler_params=_params("arbitrary", "arbitrary"),
        name="moe_up",
    )(tile_e, n_used, xs, w_gate, w_up, *r_ops)
    return outs[0], outs[1:]


def _moe_down_kernel(tile_e_ref, nused_ref, a_ref, w_ref, o_ref):
    del tile_e_ref

    @pl.when(pl.program_id(0) < nused_ref[0])
    def _():
        y = jnp.dot(a_ref[...], w_ref[...], preferred_element_type=F32)
        o_ref[...] = _pack_bf16_pairs(y)

    @pl.when(pl.program_id(0) >= nused_ref[0])
    def _():
        o_ref[...] = jnp.zeros_like(o_ref)


def moe_down(act, tile_e, n_used, w_down, layer):
    p, f = act.shape
    d = w_down.shape[3]
    row_tile = lambda i, te, nu: (jnp.minimum(i, nu[0] - 1), 0)
    return pl.pallas_call(
        _moe_down_kernel,
        grid_spec=pltpu.PrefetchScalarGridSpec(
            num_scalar_prefetch=2, grid=(p // MOE_TILE,),
            in_specs=[pl.BlockSpec((MOE_TILE, f), row_tile),
                      pl.BlockSpec((None, None, f, d), lambda i, te, nu: (layer, te[i], 0, 0))],
            out_specs=pl.BlockSpec((MOE_TILE, d // 2), lambda i, te, nu: (i, 0))),
        out_shape=jax.ShapeDtypeStruct((p, d // 2), jnp.uint32),
        compiler_params=_params("arbitrary"),
        name="moe_down",
    )(tile_e, n_used, act, w_down)


def _moe_combine_kernel(slot_ref, h_ref, p_ref, g_ref, ys_ref, h_out_ref, n_out_ref, buf_ref,
                        sems):
    i = pl.program_id(0)
    tr = h_ref.shape[0]
    half = ys_ref.shape[1]

    def start_gather(tile, b):
        def issue(k, carry):
            for kk in range(TOP_K):
                slot = slot_ref[TOP_K * (tile * tr + k) + kk]
                _row_copy(ys_ref, slot, buf_ref.at[b, kk], k, sems.at[b]).start()
            return carry
        lax.fori_loop(0, tr, issue, 0, unroll=4)

    @pl.when(i == 0)
    def _():
        start_gather(0, 0)

    @pl.when(i + 1 < pl.num_programs(0))
    def _():
        start_gather(i + 1, (i + 1) % 2)

    b = i % 2
    for kk in range(TOP_K):
        _row_copy(ys_ref, 0, buf_ref.at[b, kk], 0, sems.at[b], tr).wait()

    p1 = p_ref[:, 0:1]
    p2 = p_ref[:, 1:2]
    hi1, lo1 = _unpack_bf16_pairs(buf_ref[b, 0])
    hi2, lo2 = _unpack_bf16_pairs(buf_ref[b, 1])
    left = h_ref[:, :half] + (p1 * hi1 + p2 * hi2)
    right = h_ref[:, half:] + (p1 * lo1 + p2 * lo2)
    if h_out_ref is not None:
        h_out_ref[:, :half] = left
        h_out_ref[:, half:] = right
    if n_out_ref is not None:
        ms = (jnp.sum(left * left, axis=-1, keepdims=True)
              + jnp.sum(right * right, axis=-1, keepdims=True)) / (2 * half)
        scale = lax.rsqrt(ms + EPS)
        n_out_ref[:, :half] = (left * scale * g_ref[:, :half]).astype(n_out_ref.dtype)
        n_out_ref[:, half:] = (right * scale * g_ref[:, half:]).astype(n_out_ref.dtype)


def moe_combine(h, slots, ys, probs, next_g, final, tr=256):
    n, d = h.shape
    width = ys.shape[1]
    row_block = pl.BlockSpec((tr, d), lambda i, s: (i, 0))
    norm_shape = jax.ShapeDtypeStruct((n, d), F32 if final else BF16)
    out_specs = [row_block] if final else [row_block, row_block]
    out_shape = [norm_shape] if final else [jax.ShapeDtypeStruct((n, d), F32), norm_shape]

    def body(slot_ref, h_ref, p_ref, g_ref, ys_ref, *rest):
        outs, (buf_ref, sems) = rest[:-2], rest[-2:]
        h_out_ref, n_out_ref = (None, outs[0]) if final else outs
        _moe_combine_kernel(slot_ref, h_ref, p_ref, g_ref, ys_ref, h_out_ref, n_out_ref,
                            buf_ref, sems)

    outs = pl.pallas_call(
        body,
        grid_spec=pltpu.PrefetchScalarGridSpec(
            num_scalar_prefetch=1, grid=(n // tr,),
            in_specs=[row_block,
                      pl.BlockSpec((tr, TOP_K), lambda i, s: (i, 0)),
                      pl.BlockSpec((1, d), lambda i, s: (0, 0)),
                      pl.BlockSpec(memory_space=pl.ANY)],
            out_specs=out_specs,
            scratch_shapes=[pltpu.VMEM((2, TOP_K, tr, width), ys.dtype),
                            pltpu.SemaphoreType.DMA((2,))]),
        out_shape=out_shape,
        input_output_aliases={1: 0},
        compiler_params=_params("arbitrary"),
        name="moe_combine",
    )(slots, h, probs, next_g.reshape(1, d), ys)
    return (None, outs[0]) if final else outs


def moe_ffn(h, norm_g, router_w, router_b, w_gate, w_up, w_down, layer, next_g, final,
            round_pairs=()):
    rows, codes, probs, counts = rmsnorm_router(h, norm_g, router_w, router_b)
    xs, slots, tile_e, n_used = moe_dispatch(codes, counts, rows)
    act, rounded = moe_up(xs, tile_e, n_used, w_gate, w_up, layer, round_pairs)
    ys = moe_down(act, tile_e, n_used, w_down, layer)
    return moe_combine(h, slots, ys, probs, next_g, final) + (rounded,)


def _matmul_cast_weight_kernel(x_ref, w_ref, o_ref, wb_ref):
    @pl.when(pl.program_id(1) == 0)
    def _():
        wb_ref[...] = w_ref[...].astype(BF16)

    o_ref[...] = jnp.dot(x_ref[...], wb_ref[...],
                         preferred_element_type=F32).astype(o_ref.dtype)


def matmul_cols(x, w, layer, n_cols, out_dtype, tm=1024, tn=768):
    m, k = x.shape
    return pl.pallas_call(
        _matmul_cast_weight_kernel,
        grid=(n_cols // tn, m // tm),
        in_specs=[pl.BlockSpec((tm, k), lambda j, i: (i, 0)),
                  pl.BlockSpec((None, k, tn), lambda j, i: (layer, 0, j))],
        out_specs=pl.BlockSpec((tm, tn), lambda j, i: (i, j)),
        out_shape=jax.ShapeDtypeStruct((m, n_cols), out_dtype),
        scratch_shapes=[pltpu.VMEM((k, tn), BF16)],
        compiler_params=_params("arbitrary", "arbitrary"),
        name="in_proj",
    )(x, w)


def _matmul_residual_kernel(x_ref, w_ref, r_ref, o_ref):
    o_ref[...] = r_ref[...] + jnp.dot(x_ref[...], w_ref[...], preferred_element_type=F32)


def matmul_residual(x, w, layer, res, tm, tn, in_place=True):
    m, k = x.shape
    n = w.shape[2]
    return pl.pallas_call(
        _matmul_residual_kernel,
        grid=(n // tn, m // tm),
        in_specs=[pl.BlockSpec((tm, k), lambda j, i: (i, 0)),
                  pl.BlockSpec((None, k, tn), lambda j, i: (layer, 0, j)),
                  pl.BlockSpec((tm, tn), lambda j, i: (i, j))],
        out_specs=pl.BlockSpec((tm, tn), lambda j, i: (i, j)),
        out_shape=jax.ShapeDtypeStruct((m, n), F32),
        input_output_aliases={2: 0} if in_place else {},
        compiler_params=_params("arbitrary", "arbitrary"),
        name="matmul_residual",
    )(x, w, res)


def _gelu(x):
    return 0.5 * x * (1.0 + lax.erf(x * (2.0 ** -0.5)))


def _shift_rows(x, k):
    return pltpu.roll(x, k, axis=0)


def _mixer_kernel(p_ref, poolw_ref, pscale_ref, sgunorm_ref, ws_ref, sb_ref, convw_ref,
                  y_ref, halo_ref, *, ts):
    w = BRANCH_WIDTH
    s_idx = pl.program_id(1)

    @pl.when(s_idx == 0)
    def _():
        halo_ref[...] = jnp.zeros_like(halo_ref)

    a = p_ref[:, 0:w].astype(F32)
    ext_a = jnp.concatenate([halo_ref[:, 0:w], a], axis=0)
    pos = s_idx * ts + lax.broadcasted_iota(jnp.int32, (ts, 1), 0)
    for g, win in enumerate(POOL_WINDOWS):
        cols = slice(g * POOL_GROUP_DIM, (g + 1) * POOL_GROUP_DIM)
        acc = ext_a[:, cols]
        step = 1
        while step < win:
            acc = acc + _shift_rows(acc, step)
            step *= 2
        cnt = jnp.minimum(pos + 1, win).astype(F32)
        pooled = acc[HALO_ROWS:, :] / cnt - a[:, cols]
        ya = jnp.dot(pooled.astype(BF16), poolw_ref[g], preferred_element_type=F32)
        y_ref[:, cols] = (ya * pscale_ref[:, cols]).astype(y_ref.dtype)

    u = _gelu(p_ref[:, w:2 * w].astype(F32))
    v = _rms(_gelu(p_ref[:, 2 * w:3 * w].astype(F32)), sgunorm_ref[...]).astype(BF16)
    row = lax.broadcasted_iota(jnp.int32, (SGU_CHUNK, SGU_CHUNK), 0)
    col = lax.broadcasted_iota(jnp.int32, (SGU_CHUNK, SGU_CHUNK), 1)
    for g in range(SGU_HEADS):
        cols = slice(g * SGU_HEAD_DIM, (g + 1) * SGU_HEAD_DIM)
        w_causal = jnp.where(row >= col, ws_ref[g], 0.0).astype(BF16)
        bias = sb_ref[:, g:g + 1]
        for c in range(ts // SGU_CHUNK):
            rows = slice(c * SGU_CHUNK, (c + 1) * SGU_CHUNK)
            vm = jnp.dot(w_causal, v[rows, cols], preferred_element_type=F32) + bias
            y_ref[rows, w + g * SGU_HEAD_DIM:w + (g + 1) * SGU_HEAD_DIM] = (
                u[rows, cols] * vm).astype(y_ref.dtype)

    hc = p_ref[:, 5 * w:6 * w].astype(F32) * p_ref[:, 3 * w:4 * w].astype(F32)
    ext_h = jnp.concatenate([halo_ref[:, w:2 * w], hc], axis=0)
    yc = (convw_ref[2:3, :] * ext_h
          + convw_ref[1:2, :] * _shift_rows(ext_h, 1)
          + convw_ref[0:1, :] * _shift_rows(ext_h, 2))
    y_ref[:, 2 * w:3 * w] = (p_ref[:, 4 * w:5 * w].astype(F32)
                             * yc[HALO_ROWS:, :]).astype(y_ref.dtype)

    halo_ref[:, 0:w] = a[ts - HALO_ROWS:, :]
    halo_ref[:, w:2 * w] = hc[ts - HALO_ROWS:, :]


def token_mixers(p6, pool_w, pool_scale, sgu_norm, sgu_ws, sgu_b, conv_w, bsz, seq, ts=256):
    n = p6.shape[0]
    w = BRANCH_WIDTH
    tiles = seq // ts
    row_block = lambda b, s: (b * tiles + s, 0)
    const2 = lambda b, s: (0, 0)
    const3 = lambda b, s: (0, 0, 0)
    return pl.pallas_call(
        functools.partial(_mixer_kernel, ts=ts),
        grid=(bsz, tiles),
        in_specs=[pl.BlockSpec((ts, MIX_COLS), row_block),
                  pl.BlockSpec((POOL_GROUPS, POOL_GROUP_DIM, POOL_GROUP_DIM), const3),
                  pl.BlockSpec((1, w), const2),
                  pl.BlockSpec((1, w), const2),
                  pl.BlockSpec((SGU_HEADS, SGU_CHUNK, SGU_CHUNK), const3),
                  pl.BlockSpec((SGU_CHUNK, SGU_HEADS), const2),
                  pl.BlockSpec((CONV_WIDTH, w), const2)],
        out_specs=pl.BlockSpec((ts, N_BRANCHES * w), row_block),
        out_shape=jax.ShapeDtypeStruct((n, N_BRANCHES * w), BF16),
        scratch_shapes=[pltpu.VMEM((HALO_ROWS, 2 * w), F32)],
        compiler_params=_params("arbitrary", "arbitrary"),
    )(p6, pool_w.astype(BF16), pool_scale.reshape(1, w), sgu_norm.reshape(1, w),
      sgu_ws, sgu_b.T, conv_w)


LANES = 128


def _gate_merge_kernel(xn_ref, y_ref, wga_ref, wgb_ref, wgc_ref, wb_ref, *rest, round_blocks):
    w = BRANCH_WIDTH
    o_ref = rest[-2] if round_blocks else rest[-1]
    xn = xn_ref[...]
    acc = None
    for n, wg_ref in enumerate((wga_ref, wgb_ref, wgc_ref)):
        gate = jnp.dot(xn, wg_ref[...], preferred_element_type=F32)
        branch = jnp.dot(y_ref[:, n * w:(n + 1) * w], wb_ref[n], preferred_element_type=F32)
        term = jax.nn.sigmoid(gate) * branch
        acc = term if acc is None else acc + term
    o_ref[...] = acc.astype(o_ref.dtype)

    if round_blocks:
        next_w_ref, next_o_ref = rest[0], rest[-1]

        @pl.when(pl.program_id(1) < round_blocks)
        def _():
            next_o_ref[...] = next_w_ref[...].astype(BF16)


def gate_merge(xn, y, w_gates, w_branch, layer, w_in=None, tm=512, tn=512):
    m, d = xn.shape
    w = BRANCH_WIDTH
    gate_stride = D_MODEL // tn
    n_col, n_row = D_MODEL // tn, m // tm
    gate_cols = N_BRANCHES * D_MODEL

    def gate_spec(n):
        return pl.BlockSpec((d, tn), lambda j, i: (0, n * gate_stride + j))

    in_specs = [pl.BlockSpec((tm, d), lambda j, i: (i, 0)),
                pl.BlockSpec((tm, N_BRANCHES * w), lambda j, i: (i, 0)),
                gate_spec(0), gate_spec(1), gate_spec(2),
                pl.BlockSpec((None, N_BRANCHES, w, tn), lambda j, i: (layer, 0, 0, j))]
    out_specs = [pl.BlockSpec((tm, tn), lambda j, i: (i, j))]
    out_shape = [jax.ShapeDtypeStruct((m, D_MODEL), BF16)]
    operands = [xn, y, w_gates, w_gates, w_gates, w_branch]
    round_blocks = 0
    if w_in is not None:
        round_blocks = gate_cols // LANES // n_col
        assert round_blocks * n_col * LANES == gate_cols and round_blocks <= n_row
        block = lambda j, i: j * round_blocks + jnp.minimum(i, round_blocks - 1)
        first = MIX_COLS // LANES
        in_specs.append(pl.BlockSpec((None, d, LANES),
                                     lambda j, i: (layer + 1, 0, first + block(j, i))))
        out_specs.append(pl.BlockSpec((d, LANES), lambda j, i: (0, block(j, i))))
        out_shape.append(jax.ShapeDtypeStruct((d, gate_cols), BF16))
        operands.append(w_in)
    outs = pl.pallas_call(
        functools.partial(_gate_merge_kernel, round_blocks=round_blocks),
        grid=(n_col, n_row),
        in_specs=in_specs,
        out_specs=out_specs,
        out_shape=out_shape,
        compiler_params=_params("arbitrary", "arbitrary"),
        name="gate_merge",
    )(*operands)
    return outs if round_blocks else (outs[0], None)


ROUND_STEPS = 128


def _swiglu_up_kernel(x_ref, wg_ref, wu_ref, *rest, n_round):
    round_in, o_ref, round_out = rest[:n_round], rest[n_round], rest[n_round + 1:]
    x = x_ref[...]
    gate = jnp.dot(x, wg_ref[...], preferred_element_type=F32)
    up = jnp.dot(x, wu_ref[...], preferred_element_type=F32)
    o_ref[...] = (jax.nn.silu(gate) * up).astype(o_ref.dtype)

    if n_round:
        step = pl.program_id(0) * pl.num_programs(1) + pl.program_id(1)

        @pl.when(step < ROUND_STEPS)
        def _():
            for src_ref, dst_ref in zip(round_in, round_out):
                dst_ref[...] = src_ref[...].astype(BF16)


def swiglu_up(x, w_gate, w_up, layer, round_weights=(), round_layer=0, tm=1024, tn=512):
    m, k = x.shape
    n = w_gate.shape[2]
    n_col, n_row = n // tn, m // tm
    assert n_col * n_row >= ROUND_STEPS
    w_spec = pl.BlockSpec((None, k, tn), lambda j, i: (layer, 0, j))
    slab = lambda j, i: jnp.minimum(j * n_row + i, ROUND_STEPS - 1)
    in_specs = [pl.BlockSpec((tm, k), lambda j, i: (i, 0)), w_spec, w_spec]
    out_specs = [pl.BlockSpec((tm, tn), lambda j, i: (i, j))]
    out_shape = [jax.ShapeDtypeStruct((m, n), BF16)]
    operands = [x, w_gate, w_up]
    for w in round_weights:
        cols = w.shape[-1]
        rows = w[0].size // cols
        slab_rows = rows // ROUND_STEPS
        assert slab_rows * ROUND_STEPS == rows and slab_rows % 16 == 0
        in_specs.append(pl.BlockSpec((slab_rows, cols),
                                     lambda j, i: (round_layer * ROUND_STEPS + slab(j, i), 0)))
        out_specs.append(pl.BlockSpec((slab_rows, cols), lambda j, i: (slab(j, i), 0)))
        out_shape.append(jax.ShapeDtypeStruct((rows, cols), BF16))
        operands.append(w.reshape(w.shape[0] * rows, cols))
    outs = pl.pallas_call(
        functools.partial(_swiglu_up_kernel, n_round=len(round_weights)),
        grid=(n_col, n_row),
        in_specs=in_specs,
        out_specs=out_specs,
        out_shape=out_shape,
        compiler_params=_params("arbitrary", "arbitrary"),
        name="swiglu_up",
    )(*operands)
    return outs[0], outs[1:]


def kernel(x, norm_mix, w_in, pool_w, pool_scale, sgu_norm, sgu_ws, sgu_b, conv_w, w_branch, w_out, norm_ffn, dense_w_gate, dense_w_up, dense_w_down, router_w, router_b, moe_w_gate, moe_w_up, moe_w_down, norm_final):
    bsz, seq, d = x.shape
    h = x.reshape(bsz * seq, d)
    w_gates = w_in[0, :, MIX_COLS:].astype(BF16)
    w_branch = w_branch.astype(BF16)
    first_moe = 1
    early = first_moe + 1
    dense_stacks = (dense_w_gate, dense_w_up, dense_w_down)
    n_dense = dense_w_gate.shape[0]
    w_out_early = w_out[:early].astype(BF16)
    w_out_bf = {l: (w_out_early, l) for l in range(early)}
    dense_bf = {0: (tuple(w[:1].astype(BF16) for w in dense_stacks), 0)}
    late_pairs = ([(w, j) for j in range(1, n_dense) for w in dense_stacks]
                  + [(w_out, l) for l in range(early, DEPTH)])
    expert_w = None
    xn = None
    for layer in range(DEPTH):
        last = layer == DEPTH - 1
        if xn is None:
            xn = rmsnorm(h, norm_mix[layer], BF16)
        p6 = matmul_cols(xn, w_in, layer, MIX_COLS, BF16)
        y = token_mixers(p6, pool_w[layer], pool_scale[layer], sgu_norm[layer], sgu_ws[layer],
                         sgu_b[layer], conv_w[layer], bsz, seq)
        merged, w_gates = gate_merge(xn, y, w_gates, w_branch, layer,
                                     w_in=None if last else w_in)
        w_out_l, idx = w_out_bf[layer]
        h = matmul_residual(merged, w_out_l, idx, h, tm=1024, tn=1024, in_place=layer > 0)
        xn = None
        j = layer // 2
        if layer % 2 == 0:
            hn = rmsnorm(h, norm_ffn[layer], BF16)
            (dense_gate, dense_up, dense_down), idx = dense_bf[j]
            stacks = (moe_w_gate, moe_w_up, moe_w_down) if layer + 1 < DEPTH else ()
            act, rounded = swiglu_up(hn, dense_gate, dense_up, idx, round_weights=stacks,
                                     round_layer=j)
            expert_w = [r.reshape((1,) + s.shape[1:]) for r, s in zip(rounded, stacks)]
            h = matmul_residual(act, dense_down, idx, h, tm=512, tn=1024)
        else:
            next_g = norm_final if last else norm_mix[layer + 1]
            pairs = late_pairs if layer == first_moe else ()
            h, xn, rounded = moe_ffn(h, norm_ffn[layer], router_w[j], router_b[j],
                                     *expert_w, 0, next_g, final=last, round_pairs=pairs)
            if pairs:
                as_layer = [r.reshape((1,) + w.shape[1:]) for r, (w, _) in zip(rounded, pairs)]
                n_late = len(dense_stacks) * (n_dense - 1)
                for jj in range(1, n_dense):
                    dense_bf[jj] = (tuple(as_layer[3 * (jj - 1):3 * jj]), 0)
                for k, l in enumerate(range(early, DEPTH)):
                    w_out_bf[l] = (as_layer[n_late + k], 0)
    out = xn if DEPTH % 2 == 0 else rmsnorm(h, norm_final, F32)
    return out.reshape(bsz, seq, d)
```
